```python
import jax, jax.numpy as jnp
from jax import lax
import numpy as np

D_MODEL = 1024
BATCH = 4
SEQ = 8192
DEPTH = 4

GRID_W = 64
CTX_LEN = 256
HEAD_DIM = 64
A_HEADS = 8
A_KV_HEADS = 2
A_GROUP = A_HEADS // A_KV_HEADS
B_HEADS = 4
C_CHANNELS = 256
CONV_WIDTH = 31
NA_WIN_ROWS = 8
NA_WIN_COLS = 16
Q_BLOCK = 128
N_EXPERTS = 16
EXPERT_DIM = 2048
CAPACITY_FACTOR = 2
ROPE_THETA = 10000.0
EPS = 1e-6
A_WIDTH = A_HEADS * HEAD_DIM
A_KV_WIDTH = A_KV_HEADS * HEAD_DIM
B_WIDTH = B_HEADS * HEAD_DIM
MIX_WIDTH = A_WIDTH + B_WIDTH + C_CHANNELS
IN_WIDTH = A_WIDTH + 2 * A_KV_WIDTH + 3 * B_WIDTH + 2 * C_CHANNELS

kernel_name = "hybrid_gqa_natten_conformer_ecmoe_dit"


def rms_norm(x, g):
    xf = x.astype(jnp.float32)
    y = xf * lax.rsqrt(jnp.mean(xf * xf, axis=-1, keepdims=True) + EPS)
    return (y * g.astype(jnp.float32)).astype(x.dtype)


def layer_norm(x, g, b):
    xf = x.astype(jnp.float32)
    mu = jnp.mean(xf, axis=-1, keepdims=True)
    var = jnp.mean(jnp.square(xf - mu), axis=-1, keepdims=True)
    y = (xf - mu) * lax.rsqrt(var + EPS)
    return (y * g.astype(jnp.float32) + b.astype(jnp.float32)).astype(x.dtype)


def modulate(h, shift, scale):
    return h * (1 + scale) + shift


def rope_axis(x, pos):
    half = x.shape[-1] // 2
    inv = ROPE_THETA ** (-jnp.arange(half, dtype=jnp.float32) / half)
    ang = pos.astype(jnp.float32)[:, None] * inv[None, :]
    cos = jnp.cos(ang)[:, None, :]
    sin = jnp.sin(ang)[:, None, :]
    xf = x.astype(jnp.float32)
    x1, x2 = xf[..., :half], xf[..., half:]
    return jnp.concatenate([x1 * cos - x2 * sin, x2 * cos + x1 * sin], axis=-1).astype(x.dtype)


def rope_2d(x, pos_row, pos_col):
    h = x.shape[-1] // 2
    return jnp.concatenate([rope_axis(x[..., :h], pos_row), rope_axis(x[..., h:], pos_col)], axis=-1)


def gqa_attend(q, k, v):
    b, nq = q.shape[0], q.shape[1]
    s = jnp.einsum('bqkgd,btkd->bkgqt', q, k).astype(jnp.float32) * (q.shape[-1] ** -0.5)
    p = jax.nn.softmax(s, axis=-1).astype(v.dtype)
    o = jnp.einsum('bkgqt,btkd->bqkgd', p, v)
    return o.reshape(b, nq, -1)


def axial_gqa_mixer(qa_l, ka_l, va_l, qa_c, ka_c, va_c, q_norm_g, k_norm_g, pos_row, pos_col, ctx_queries):
    b, n, _ = qa_l.shape
    nc = ka_c.shape[1]
    q_l = rope_2d(rms_norm(qa_l.reshape(b, n, A_HEADS, HEAD_DIM), q_norm_g), pos_row, pos_col)
    k_l = rope_2d(rms_norm(ka_l.reshape(b, n, A_KV_HEADS, HEAD_DIM), k_norm_g), pos_row, pos_col)
    v_l = va_l.reshape(b, n, A_KV_HEADS, HEAD_DIM)
    k_c = rms_norm(ka_c.reshape(b, nc, A_KV_HEADS, HEAD_DIM), k_norm_g)
    v_c = va_c.reshape(b, nc, A_KV_HEADS, HEAD_DIM)
    k_all = jnp.concatenate([k_c, k_l], axis=1)
    v_all = jnp.concatenate([v_c, v_l], axis=1)
    q_blocks = jnp.moveaxis(q_l.reshape(b, n // Q_BLOCK, Q_BLOCK, A_KV_HEADS, A_GROUP, HEAD_DIM), 1, 0)
    out = lax.map(lambda qb: gqa_attend(qb, k_all, v_all), q_blocks)
    a_lat = jnp.moveaxis(out, 0, 1).reshape(b, n, A_WIDTH)
    a_ctx = None
    if ctx_queries:
        q_c = rms_norm(qa_c.reshape(b, nc, A_HEADS, HEAD_DIM), q_norm_g)
        a_ctx = gqa_attend(q_c.reshape(b, nc, A_KV_HEADS, A_GROUP, HEAD_DIM), k_c, v_c)
    return a_lat, a_ctx


def neighbourhood_mixer(qb_l, kb_l, vb_l, qb_c, kb_c, vb_c, rpb, ctx_queries):
    b, n, _ = qb_l.shape
    nc = kb_c.shape[1]
    rows = n // GRID_W
    wr = min(NA_WIN_ROWS, rows)
    wc = NA_WIN_COLS
    scale = HEAD_DIM ** -0.5
    qg = qb_l.reshape(b, rows, GRID_W, B_HEADS, HEAD_DIM)
    kg = kb_l.reshape(b, rows, GRID_W, B_HEADS, HEAD_DIM)
    vg = vb_l.reshape(b, rows, GRID_W, B_HEADS, HEAD_DIM)
    k_c = kb_c.reshape(b, nc, B_HEADS, HEAD_DIM)
    v_c = vb_c.reshape(b, nc, B_HEADS, HEAD_DIM)
    col = jnp.arange(GRID_W)
    col_start = jnp.clip(col - wc // 2, 0, GRID_W - wc)
    col_idx = col_start[:, None] + jnp.arange(wc)[None, :]
    dc = col_idx - col[:, None] + (NA_WIN_COLS - 1)

    def row_block(args):
        r, q_row = args
        rs = jnp.clip(r - wr // 2, 0, rows - wr)
        k_rows = lax.dynamic_slice_in_dim(kg, rs, wr, axis=1)
        v_rows = lax.dynamic_slice_in_dim(vg, rs, wr, axis=1)
        k_nb = k_rows[:, :, col_idx]
        v_nb = v_rows[:, :, col_idx]
        dr = rs + jnp.arange(wr) - r + (NA_WIN_ROWS - 1)
        bias = rpb[:, dr[None, :, None], dc[:, None, :]].astype(jnp.float32)
        s_nb = jnp.einsum('bqhd,bwqjhd->bhqwj', q_row, k_nb).astype(jnp.float32) * scale + bias[None]
        s_nb = s_nb.reshape(b, B_HEADS, GRID_W, wr * wc)
        s_cx = jnp.einsum('bqhd,bchd->bhqc', q_row, k_c).astype(jnp.float32) * scale
        p = jax.nn.softmax(jnp.concatenate([s_nb, s_cx], axis=-1), axis=-1).astype(v_c.dtype)
        p_nb = p[..., :wr * wc].reshape(b, B_HEADS, GRID_W, wr, wc)
        p_cx = p[..., wr * wc:]
        return (jnp.einsum('bhqwj,bwqjhd->bqhd', p_nb, v_nb)
                + jnp.einsum('bhqc,bchd->bqhd', p_cx, v_c))

    out = lax.map(row_block, (jnp.arange(rows), jnp.moveaxis(qg, 1, 0)))
    b_lat = jnp.moveaxis(out, 0, 1).reshape(b, n, B_WIDTH)
    b_ctx = None
    if ctx_queries:
        q_c = qb_c.reshape(b, nc, B_HEADS, 1, HEAD_DIM)
        b_ctx = gqa_attend(q_c, k_c, v_c)
    return b_lat, b_ctx


def conformer_conv(val, gate, conv_w, conv_b, ln_g, ln_b):
    u = val * jax.nn.sigmoid(gate)
    y = lax.conv_general_dilated(
        u, conv_w[:, None, :].astype(u.dtype), window_strides=(1,),
        padding=[(CONV_WIDTH // 2, CONV_WIDTH // 2)],
        dimension_numbers=('NWC', 'WIO', 'NWC'), feature_group_count=C_CHANNELS)
    y = y + conv_b
    return jax.nn.silu(layer_norm(y, ln_g, ln_b))


def expert_choice_moe(h, w_router, w_gate, w_up, w_down):
    b, n, d = h.shape
    cap = CAPACITY_FACTOR * n // N_EXPERTS
    aff = jax.nn.softmax(jnp.einsum('bnd,de->bne', h, w_router).astype(jnp.float32), axis=-1)
    gate, idx = lax.top_k(jnp.swapaxes(aff, 1, 2), cap)

    def one_expert(args):
        wg, wu, wd, idx_e, gate_e = args
        xe = jax.vmap(lambda hb, ib: hb[ib])(h, idx_e)
        ye = (jax.nn.silu(xe @ wg) * (xe @ wu)) @ wd
        return ye * gate_e[..., None].astype(ye.dtype)

    y = lax.map(one_expert, (w_gate, w_up, w_down, jnp.swapaxes(idx, 0, 1), jnp.swapaxes(gate, 0, 1)))
    y = jnp.swapaxes(y, 0, 1).reshape(b, N_EXPERTS * cap, d)
    idx_flat = idx.reshape(b, N_EXPERTS * cap)
    return jax.vmap(lambda yb, ib: jnp.zeros((n, d), yb.dtype).at[ib].add(yb))(y, idx_flat)


def split_in_proj(p):
    sizes = (A_WIDTH, A_KV_WIDTH, A_KV_WIDTH, B_WIDTH, B_WIDTH, B_WIDTH, C_CHANNELS, C_CHANNELS)
    offsets = [int(v) for v in np.cumsum(sizes)[:-1]]
    return jnp.split(p, offsets, axis=-1)


def hybrid_layer(x_lat, x_ctx, mod_lat, mod_ctx, pos_row, pos_col, norm1_g, norm2_g, w_in, q_norm_g, k_norm_g,
                 na_rpb, conv_w, conv_b, conv_ln_g, conv_ln_b, w_out, w_router, w_gate, w_up, w_down, last):
    s1, sc1, g1, s2, sc2, g2 = jnp.split(mod_lat, 6, axis=-1)
    cs1, csc1, cg1, cs2, csc2, cg2 = jnp.split(mod_ctx, 6, axis=-1)
    ctx_queries = not last

    h_lat = modulate(rms_norm(x_lat, norm1_g), s1, sc1)
    h_ctx = modulate(rms_norm(x_ctx, norm1_g), cs1, csc1)
    qa_l, ka_l, va_l, qb_l, kb_l, vb_l, cv_l, cg_l = split_in_proj(h_lat @ w_in)
    qa_c, ka_c, va_c, qb_c, kb_c, vb_c, cv_c, cg_c = split_in_proj(h_ctx @ w_in)

    a_lat, a_ctx = axial_gqa_mixer(qa_l, ka_l, va_l, qa_c, ka_c, va_c, q_norm_g, k_norm_g,
                                   pos_row, pos_col, ctx_queries)
    b_lat, b_ctx = neighbourhood_mixer(qb_l, kb_l, vb_l, qb_c, kb_c, vb_c, na_rpb, ctx_queries)
    c_lat = conformer_conv(cv_l, cg_l, conv_w, conv_b, conv_ln_g, conv_ln_b)

    x_lat = x_lat + g1 * (jnp.concatenate([a_lat, b_lat, c_lat], axis=-1) @ w_out)
    h2 = modulate(rms_norm(x_lat, norm2_g), s2, sc2)
    x_lat = x_lat + g2 * expert_choice_moe(h2, w_router, w_gate, w_up, w_down)

    if ctx_queries:
        c_ctx_out = conformer_conv(cv_c, cg_c, conv_w, conv_b, conv_ln_g, conv_ln_b)
        x_ctx = x_ctx + cg1 * (jnp.concatenate([a_ctx, b_ctx, c_ctx_out], axis=-1) @ w_out)
        hc2 = modulate(rms_norm(x_ctx, norm2_g), cs2, csc2)
        x_ctx = x_ctx + cg2 * expert_choice_moe(hc2, w_router, w_gate, w_up, w_down)
    return x_lat, x_ctx


def setup_inputs(seed: int = 0) -> dict:
    key = jax.random.key(seed)
    ks = jax.random.split(key, 24)
    f32 = jnp.float32

    def nrm(k, shape, scale):
        return jax.random.normal(k, shape, f32) * scale

    d = D_MODEL
    return {
        "x": nrm(ks[0], (BATCH, SEQ, d), 1.0),
        "c": nrm(ks[1], (BATCH, d), 1.0),
        "ctx": nrm(ks[2], (BATCH, CTX_LEN, d), 1.0),
        "c_ctx": nrm(ks[3], (d,), 1.0),
        "w_ada": nrm(ks[4], (DEPTH, d, 6 * d), 0.5 * d ** -0.5),
        "b_ada": nrm(ks[5], (DEPTH, 6 * d), 0.02),
        "norm1_g": 1.0 + nrm(ks[6], (DEPTH, d), 0.05),
        "norm2_g": 1.0 + nrm(ks[7], (DEPTH, d), 0.05),
        "w_in": nrm(ks[8], (DEPTH, d, IN_WIDTH), d ** -0.5),
        "q_norm_g": 1.0 + nrm(ks[9], (DEPTH, HEAD_DIM), 0.05),
        "k_norm_g": 1.0 + nrm(ks[10], (DEPTH, HEAD_DIM), 0.05),
        "na_rpb": nrm(ks[11], (DEPTH, B_HEADS, 2 * NA_WIN_ROWS - 1, 2 * NA_WIN_COLS - 1), 0.1),
        "conv_w": nrm(ks[12], (DEPTH, CONV_WIDTH, C_CHANNELS), CONV_WIDTH ** -0.5),
        "conv_b": nrm(ks[13], (DEPTH, C_CHANNELS), 0.02),
        "conv_ln_g": 1.0 + nrm(ks[14], (DEPTH, C_CHANNELS), 0.05),
        "conv_ln_b": nrm(ks[15], (DEPTH, C_CHANNELS), 0.02),
        "w_out": nrm(ks[16], (DEPTH, MIX_WIDTH, d), MIX_WIDTH ** -0.5),
        "w_router": nrm(ks[17], (DEPTH, d, N_EXPERTS), d ** -0.5),
        "w_gate": nrm(ks[18], (DEPTH, N_EXPERTS, d, EXPERT_DIM), d ** -0.5),
        "w_up": nrm(ks[19], (DEPTH, N_EXPERTS, d, EXPERT_DIM), d ** -0.5),
        "w_down": nrm(ks[20], (DEPTH, N_EXPERTS, EXPERT_DIM, d), EXPERT_DIM ** -0.5),
        "final_norm_g": 1.0 + nrm(ks[21], (d,), 0.05),
    }


def reference(x, c, ctx, c_ctx, w_ada, b_ada, norm1_g, norm2_g, w_in, q_norm_g, k_norm_g, na_rpb,
              conv_w, conv_b, conv_ln_g, conv_ln_b, w_out, w_router, w_gate, w_up, w_down, final_norm_g):
    n = x.shape[1]
    t = jnp.arange(n)
    pos_row = t // GRID_W
    pos_col = t % GRID_W
    silu_c = jax.nn.silu(c)
    silu_cc = jax.nn.silu(c_ctx)
    x_lat, x_ctx = x, ctx
    for i in range(DEPTH):
        mod_lat = (silu_c @ w_ada[i] + b_ada[i])[:, None, :]
        mod_ctx = (silu_cc @ w_ada[i] + b_ada[i])[None, None, :]
        x_lat, x_ctx = hybrid_layer(
            x_lat, x_ctx, mod_lat, mod_ctx, pos_row, pos_col, norm1_g[i], norm2_g[i], w_in[i],
            q_norm_g[i], k_norm_g[i], na_rpb[i], conv_w[i], conv_b[i], conv_ln_g[i], conv_ln_b[i],
            w_out[i], w_router[i], w_gate[i], w_up[i], w_down[i], i == DEPTH - 1)
    return rms_norm(x_lat, final_norm_g)
```

```python
import functools

import numpy as np
import jax
import jax.numpy as jnp
from jax import lax
from jax.experimental import pallas as pl
from jax.experimental.pallas import tpu as pltpu

F32 = jnp.float32
BF16 = jnp.bfloat16
I32 = jnp.int32

GRID_W = 64
HEAD_DIM = 64
A_HEADS = 8
A_KV_HEADS = 2
A_GROUP = A_HEADS // A_KV_HEADS
B_HEADS = 4
C_CHANNELS = 256
CONV_WIDTH = 31
NA_WIN_ROWS = 8
NA_WIN_COLS = 16
N_EXPERTS = 16
CAPACITY_FACTOR = 2
ROPE_THETA = 10000.0
EPS = 1e-6
A_WIDTH = A_HEADS * HEAD_DIM
A_KV_WIDTH = A_KV_HEADS * HEAD_DIM
B_WIDTH = B_HEADS * HEAD_DIM
MIX_WIDTH = A_WIDTH + B_WIDTH + C_CHANNELS
IN_WIDTH = A_WIDTH + 2 * A_KV_WIDTH + 3 * B_WIDTH + 2 * C_CHANNELS

LANES = 128
SUBLANES = 8
BF16_ROWS = 16
VMEM_LIMIT = 56 * 1024 * 1024

TT = 256
NEG = -1e30
CONV_HALO = 16
DISPATCH_ROUND = 64
COMBINE_WINDOW = 128
COMBINE_ROUND = COMBINE_WINDOW - BF16_ROWS


def _cparams(n_axes):
    return pltpu.CompilerParams(dimension_semantics=("arbitrary",) * n_axes, vmem_limit_bytes=VMEM_LIMIT)


def _split3(x):
    hi = x.astype(BF16)
    r1 = x - hi.astype(F32)
    mid = r1.astype(BF16)
    lo = (r1 - mid.astype(F32)).astype(BF16)
    return hi, mid, lo


def _dot(a, b):
    return jnp.dot(a, b, preferred_element_type=F32)


def _dot_nt(a, b):
    return lax.dot_general(a, b, (((1,), (1,)), ((), ())), preferred_element_type=F32)


def _dot3(a, b_hi, b_lo):
    a_hi = a.astype(BF16)
    a_lo = (a - a_hi.astype(F32)).astype(BF16)
    return _dot(a_hi, b_hi) + (_dot(a_lo, b_hi) + _dot(a_hi, b_lo))


def _ada_kernel(cond_ref, w_ref, b_ref, o_ref):
    c = cond_ref[...]
    s = c * jax.nn.sigmoid(c)
    w = w_ref[0]
    w_hi = w.astype(BF16)
    w_lo = (w - w_hi.astype(F32)).astype(BF16)
    o_ref[0] = _dot3(s, w_hi, w_lo) + b_ref[0]


def _ada(cond, w_ada, b_ada):
    depth, d, six_d = w_ada.shape
    bn = six_d // 4
    return pl.pallas_call(
        _ada_kernel,
        grid=(depth, six_d // bn),
        in_specs=[pl.BlockSpec((SUBLANES, d), lambda l, j: (0, 0)),
                  pl.BlockSpec((1, d, bn), lambda l, j: (l, 0, j)),
                  pl.BlockSpec((1, 1, bn), lambda l, j: (l, 0, j))],
        out_specs=pl.BlockSpec((1, SUBLANES, bn), lambda l, j: (l, 0, j)),
        out_shape=jax.ShapeDtypeStruct((depth, SUBLANES, six_d), F32),
        compiler_params=_cparams(2),
        name="ada",
    )(cond, w_ada, b_ada.reshape(depth, 1, six_d))


def _head_rms(x, ones_bd, g):
    sq = x * x
    hi = sq.astype(BF16)
    lo = (sq - hi.astype(F32)).astype(BF16)
    ssum = _dot(hi, ones_bd) + _dot(lo, ones_bd)
    return x * lax.rsqrt(ssum * (1.0 / HEAD_DIM) + EPS) * g


def _rope(x, cos, sin_signed):
    w = x.shape[-1]
    lane = lax.broadcasted_iota(I32, x.shape, 1)
    first_half = (lane % 32) < 16
    partner = jnp.where(first_half, pltpu.roll(x, w - 16, 1), pltpu.roll(x, 16, 1))
    return x * cos + partner * sin_signed


def _in_proj_kernel(x_ref, mod_ref, g1_ref, w_ref, cos_ref, sin_ref, qg_ref, kg_ref, onesq_ref, onesk_ref,
                    qa_ref, ka_ref, va_ref, qb_ref, kb_ref, vb_ref, u_ref, *, d):
    x = x_ref[...]
    y = x * lax.rsqrt(jnp.mean(x * x, axis=-1, keepdims=True) + EPS) * g1_ref[...]
    shift = mod_ref[:, 0:d]
    scale = mod_ref[:, d:2 * d]
    h = (y * (1.0 + scale) + shift).astype(BF16)
    p = _dot(h, w_ref[...])
    o = 0
    qa = p[:, o:o + A_WIDTH]; o += A_WIDTH
    ka = p[:, o:o + A_KV_WIDTH]; o += A_KV_WIDTH
    va = p[:, o:o + A_KV_WIDTH]; o += A_KV_WIDTH
    qb = p[:, o:o + B_WIDTH]; o += B_WIDTH
    kb = p[:, o:o + B_WIDTH]; o += B_WIDTH
    vb = p[:, o:o + B_WIDTH]; o += B_WIDTH
    cv = p[:, o:o + C_CHANNELS]; o += C_CHANNELS
    cg = p[:, o:o + C_CHANNELS]

    cos = cos_ref[...]
    sin = sin_ref[...]
    sm_scale = HEAD_DIM ** -0.5
    qa = _head_rms(qa, onesq_ref[...], qg_ref[...])
    qa = _rope(qa, jnp.concatenate([cos] * (A_WIDTH // LANES), axis=1),
               jnp.concatenate([sin] * (A_WIDTH // LANES), axis=1)) * sm_scale
    ka = _head_rms(ka, onesk_ref[...], kg_ref[...])
    ka = _rope(ka, cos, sin)
    for hh in range(A_HEADS):
        qa_ref[hh] = qa[:, hh * HEAD_DIM:(hh + 1) * HEAD_DIM].astype(BF16)
    for hh in range(A_KV_HEADS):
        ka_ref[hh] = ka[:, hh * HEAD_DIM:(hh + 1) * HEAD_DIM].astype(BF16)
        va_ref[hh] = va[:, hh * HEAD_DIM:(hh + 1) * HEAD_DIM].astype(BF16)
    qb_ref[...] = (qb * sm_scale).astype(BF16)
    kb_ref[...] = kb.astype(BF16)
    vb_ref[...] = vb.astype(BF16)
    u_ref[...] = cv * jax.nn.sigmoid(cg)


def _in_proj(x, modsel, g1, w_in, cos, sin, qg, kg, ones_q, ones_k):
    b, s, d = x.shape
    nt = s // TT
    tile = lambda bi, t: (bi, t, 0)
    head_tile = lambda bi, t: (bi, 0, t, 0)
    const2 = lambda bi, t: (0, 0)
    return pl.pallas_call(
        functools.partial(_in_proj_kernel, d=d),
        grid=(b, nt),
        in_specs=[pl.BlockSpec((None, TT, d), tile),
                  pl.BlockSpec((None, None, 1, 6 * d), lambda bi, t: (bi, jnp.minimum(t, 1), 0, 0)),
                  pl.BlockSpec((1, d), const2),
                  pl.BlockSpec((d, IN_WIDTH), const2),
                  pl.BlockSpec((TT, LANES), lambda bi, t: (t, 0)),
                  pl.BlockSpec((TT, LANES), lambda bi, t: (t, 0)),
                  pl.BlockSpec((1, A_WIDTH), const2),
                  pl.BlockSpec((1, A_KV_WIDTH), const2),
                  pl.BlockSpec((A_WIDTH, A_WIDTH), const2),
                  pl.BlockSpec((A_KV_WIDTH, A_KV_WIDTH), const2)],
        out_specs=[pl.BlockSpec((None, A_HEADS, TT, HEAD_DIM), head_tile),
                   pl.BlockSpec((None, A_KV_HEADS, TT, HEAD_DIM), head_tile),
                   pl.BlockSpec((None, A_KV_HEADS, TT, HEAD_DIM), head_tile),
                   pl.BlockSpec((None, TT, B_WIDTH), tile),
                   pl.BlockSpec((None, TT, B_WIDTH), tile),
                   pl.BlockSpec((None, TT, B_WIDTH), tile),
                   pl.BlockSpec((None, TT, C_CHANNELS), tile)],
        out_shape=[jax.ShapeDtypeStruct((b, A_HEADS, s, HEAD_DIM), BF16),
                   jax.ShapeDtypeStruct((b, A_KV_HEADS, s, HEAD_DIM), BF16),
                   jax.ShapeDtypeStruct((b, A_KV_HEADS, s, HEAD_DIM), BF16),
                   jax.ShapeDtypeStruct((b, s, B_WIDTH), BF16),
                   jax.ShapeDtypeStruct((b, s, B_WIDTH), BF16),
                   jax.ShapeDtypeStruct((b, s, B_WIDTH), BF16),
                   jax.ShapeDtypeStruct((b, s, C_CHANNELS), F32)],
        compiler_params=_cparams(2),
        name="in_proj",
    )(x, modsel, g1, w_in, cos, sin, qg, kg, ones_q, ones_k)


def _attn_a_kernel(q_ref, k_ref, v_ref, o_ref, *, ctx, tk, n_chunks):
    t = pl.program_id(2)
    g, tq, dh = q_ref.shape
    q = q_ref[...].reshape(g * tq, dh)

    def step(k, v, carry):
        m, l, acc = carry
        s = _dot_nt(q, k)
        m_new = jnp.maximum(m, jnp.max(s, axis=1, keepdims=True))
        alpha = jnp.exp(m - m_new)
        p = jnp.exp(s - m_new)
        l = alpha * l + jnp.sum(p, axis=1, keepdims=True)
        acc = alpha * acc + _dot(p.astype(BF16), v)
        return m_new, l, acc

    init = (jnp.full((g * tq, 1), NEG, F32), jnp.zeros((g * tq, 1), F32), jnp.zeros((g * tq, dh), F32))
    carry = step(k_ref[0:ctx, :], v_ref[0:ctx, :], init)

    def body(c, carry):
        start = pl.multiple_of(ctx + c * tk, int(np.gcd(ctx, tk)))
        return step(k_ref[pl.ds(start, tk), :], v_ref[pl.ds(start, tk), :], carry)

    def finish(carry):
        _, l, acc = carry
        o = (acc / l).reshape(g, tq, dh)
        o_ref[...] = jnp.concatenate([o[i] for i in range(g)], axis=1).astype(o_ref.dtype)

    @pl.when(t == 0)
    def _():
        finish(carry)

    @pl.when(t > 0)
    def _():
        finish(lax.fori_loop(0, n_chunks, body, carry))


def _attn_a(qa, ka, va, ctx):
    b, _, s, dh = qa.shape
    nt = s // TT
    tk = 1024 if (s - ctx) % 1024 == 0 else TT
    return pl.pallas_call(
        functools.partial(_attn_a_kernel, ctx=ctx, tk=tk, n_chunks=(s - ctx) // tk),
        grid=(b, A_KV_HEADS, nt),
        in_specs=[pl.BlockSpec((None, A_GROUP, TT, dh), lambda bi, j, t: (bi, j, t, 0)),
                  pl.BlockSpec((None, None, s, dh), lambda bi, j, t: (bi, j, 0, 0)),
                  pl.BlockSpec((None, None, s, dh), lambda bi, j, t: (bi, j, 0, 0))],
        out_specs=pl.BlockSpec((None, TT, A_GROUP * dh), lambda bi, j, t: (bi, t, j)),
        out_shape=jax.ShapeDtypeStruct((b, s, A_WIDTH), BF16),
        compiler_params=_cparams(3),
        name="attn_a",
    )(qa, ka, va)


def _attn_b_kernel(q_ref, k_ref, v_ref, bias_ref, o_ref, *, ctx, rows, wr):
    t = pl.program_id(1)
    tt, w = q_ref.shape
    rows_per_tile = tt // GRID_W
    win = wr * GRID_W
    q = q_ref[...]
    kc = k_ref[0:ctx, :]
    vc = v_ref[0:ctx, :]
    lane_head = lax.broadcasted_iota(I32, (tt, w), 1) // HEAD_DIM

    def softmax_parts(parts):
        m = parts[0].max(axis=1, keepdims=True)
        for s in parts[1:]:
            m = jnp.maximum(m, s.max(axis=1, keepdims=True))
        es = [jnp.exp(s - m) for s in parts]
        tot = es[0].sum(axis=1, keepdims=True)
        for e in es[1:]:
            tot = tot + e.sum(axis=1, keepdims=True)
        inv = 1.0 / tot
        return [(e * inv).astype(BF16) for e in es]

    @pl.when(t == 0)
    def _():
        out = jnp.zeros((tt, w), F32)
        for hh in range(B_HEADS):
            qh = jnp.where(lane_head == hh, q, jnp.zeros_like(q))
            (p,) = softmax_parts([_dot_nt(qh, kc)])
            out = out + jnp.where(lane_head == hh, _dot(p, vc), 0.0)
        o_ref[...] = out.astype(o_ref.dtype)

    @pl.when(t > 0)
    def _():
        out = jnp.zeros((tt, w), F32)
        r0 = (t - 1) * rows_per_tile
        for hh in range(B_HEADS):
            qh = jnp.where(lane_head == hh, q, jnp.zeros_like(q))
            s_cx = _dot_nt(qh, kc)
            s_nb = []
            v_rows = []
            for i in range(rows_per_tile):
                r = r0 + i
                rs = jnp.clip(r - wr // 2, 0, rows - wr)
                start = pl.multiple_of(ctx + rs * GRID_W, GRID_W)
                k_rows = k_ref[pl.ds(start, win), :]
                v_rows.append(v_ref[pl.ds(start, win), :])
                s_nb.append(_dot_nt(qh[i * GRID_W:(i + 1) * GRID_W], k_rows) + bias_ref[r - rs, hh])
            p_cx, p_nb = softmax_parts([s_cx, jnp.concatenate(s_nb, axis=0)])
            o_cx = _dot(p_cx, vc)
            o_nb = jnp.concatenate([_dot(p_nb[i * GRID_W:(i + 1) * GRID_W], v_rows[i])
                                    for i in range(rows_per_tile)], axis=0)
            out = out + jnp.where(lane_head == hh, o_cx + o_nb, 0.0)
        o_ref[...] = out.astype(o_ref.dtype)


def _attn_b(qb, kb, vb, bias, ctx):
    b, s, w = qb.shape
    nt = s // TT
    rows = (s - ctx) // GRID_W
    wr = min(NA_WIN_ROWS, rows)
    tile = lambda bi, t: (bi, t, 0)
    whole = lambda bi, t: (bi, 0, 0)
    return pl.pallas_call(
        functools.partial(_attn_b_kernel, ctx=ctx, rows=rows, wr=wr),
        grid=(b, nt),
        in_specs=[pl.BlockSpec((None, TT, w), tile),
                  pl.BlockSpec((None, s, w), whole),
                  pl.BlockSpec((None, s, w), whole),
                  pl.BlockSpec(bias.shape, lambda bi, t: (0, 0, 0, 0))],
        out_specs=pl.BlockSpec((None, TT, w), tile),
        out_shape=jax.ShapeDtypeStruct((b, s, w), BF16),
        compiler_params=_cparams(2),
        name="attn_b",
    )(qb, kb, vb, bias)


def _na_bias_tables(rpb, rows):
    wr = min(NA_WIN_ROWS, rows)
    wc = NA_WIN_COLS
    q = np.arange(GRID_W)
    cs = np.clip(q - wc // 2, 0, GRID_W - wc)
    kc = np.arange(GRID_W)
    valid = (kc[None, :] >= cs[:, None]) & (kc[None, :] < cs[:, None] + wc)
    dc = np.clip(kc[None, :] - q[:, None] + (NA_WIN_COLS - 1), 0, 2 * NA_WIN_COLS - 2)
    v = np.arange(wr)
    wrow = np.arange(wr)
    dr = np.clip(wrow[None, :] - v[:, None] + (NA_WIN_ROWS - 1), 0, 2 * NA_WIN_ROWS - 2)
    tab = rpb[:, dr[:, None, :, None], dc[None, :, None, :]]
    tab = jnp.where(valid[None, None, :, None, :], tab.astype(F32), NEG)
    return jnp.transpose(tab, (1, 0, 2, 3, 4)).reshape(wr, B_HEADS, GRID_W, wr * GRID_W)


def _mix_out_kernel(a_ref, b_ref, up_ref, uc_ref, un_ref, x_ref, mod_ref, cw_ref, cb_ref, lg_ref, lb_ref,
                    wo_ref, g2_ref, wrh_ref, wrl_ref,
                    x1_ref, h2_ref, aff_ref, gate_ref, *, d, nt):
    t = pl.program_id(1)
    tt = uc_ref.shape[0]
    prev_ok = t >= 2
    next_ok = jnp.logical_and(t >= 1, t < nt - 1)
    prev = jnp.where(prev_ok, up_ref[tt - CONV_HALO:tt, :], 0.0)
    nxt = jnp.where(next_ok, un_ref[0:CONV_HALO, :], 0.0)
    win = jnp.concatenate([prev, uc_ref[...], nxt], axis=0)
    off = CONV_HALO - CONV_WIDTH // 2
    acc = jnp.zeros((tt, C_CHANNELS), F32) + cb_ref[...]
    for k in range(CONV_WIDTH):
        acc = acc + win[off + k:off + k + tt, :] * cw_ref[k:k + 1, :]
    mu = jnp.mean(acc, axis=-1, keepdims=True)
    var = jnp.mean(jnp.square(acc - mu), axis=-1, keepdims=True)
    yc = (acc - mu) * lax.rsqrt(var + EPS) * lg_ref[...] + lb_ref[...]
    c = (yc * jax.nn.sigmoid(yc)).astype(BF16)

    mix = (_dot(a_ref[...], wo_ref[0:A_WIDTH, :])
           + _dot(b_ref[...], wo_ref[A_WIDTH:A_WIDTH + B_WIDTH, :])
           + _dot(c, wo_ref[A_WIDTH + B_WIDTH:MIX_WIDTH, :]))
    g1 = mod_ref[:, 2 * d:3 * d]
    x1 = x_ref[...] + g1 * mix
    x1_ref[...] = x1
    y = x1 * lax.rsqrt(jnp.mean(x1 * x1, axis=-1, keepdims=True) + EPS) * g2_ref[...]
    h2 = y * (1.0 + mod_ref[:, 4 * d:5 * d]) + mod_ref[:, 3 * d:4 * d]
    h2_ref[...] = h2.astype(BF16)
    logits = _dot3(h2, wrh_ref[...], wrl_ref[...])
    lane = lax.broadcasted_iota(I32, logits.shape, 1)
    logits = jnp.where(lane < N_EXPERTS, logits, NEG)
    e = jnp.exp(logits - jnp.max(logits, axis=-1, keepdims=True))
    aff = e / jnp.sum(e, axis=-1, keepdims=True)
    aff_ref[...] = aff.T[0:N_EXPERTS, :]
    hi, mid, lo = _split3(aff)
    gate = jnp.where(lane < N_EXPERTS, hi.astype(F32),
                     jnp.where(lane < 2 * N_EXPERTS, pltpu.roll(mid.astype(F32), N_EXPERTS, 1),
                               jnp.where(lane < 3 * N_EXPERTS, pltpu.roll(lo.astype(F32), 2 * N_EXPERTS, 1), 0.0)))
    gate_ref[...] = gate.astype(BF16)


def _mix_out(a, bmix, u, x, modsel, cw, cb, lg, lb, wo, g2, wr_hi, wr_lo):
    b, s, d = x.shape
    nt = s // TT
    tile = lambda bi, t: (bi, t, 0)
    const2 = lambda bi, t: (0, 0)
    return pl.pallas_call(
        functools.partial(_mix_out_kernel, d=d, nt=nt),
        grid=(b, nt),
        in_specs=[pl.BlockSpec((None, TT, A_WIDTH), tile),
                  pl.BlockSpec((None, TT, B_WIDTH), tile),
                  pl.BlockSpec((None, TT, C_CHANNELS), lambda bi, t: (bi, jnp.maximum(t - 1, 0), 0)),
                  pl.BlockSpec((None, TT, C_CHANNELS), tile),
                  pl.BlockSpec((None, TT, C_CHANNELS), lambda bi, t: (bi, jnp.minimum(t + 1, nt - 1), 0)),
                  pl.BlockSpec((None, TT, d), tile),
                  pl.BlockSpec((None, None, 1, 6 * d), lambda bi, t: (bi, jnp.minimum(t, 1), 0, 0)),
                  pl.BlockSpec((CONV_WIDTH, C_CHANNELS), const2),
                  pl.BlockSpec((1, C_CHANNELS), const2),
                  pl.BlockSpec((1, C_CHANNELS), const2),
                  pl.BlockSpec((1, C_CHANNELS), const2),
                  pl.BlockSpec((MIX_WIDTH, d), const2),
                  pl.BlockSpec((1, d), const2),
                  pl.BlockSpec((d, LANES), const2),
                  pl.BlockSpec((d, LANES), const2)],
        out_specs=[pl.BlockSpec((None, TT, d), tile),
                   pl.BlockSpec((None, TT, d), tile),
                   pl.BlockSpec((None, N_EXPERTS, TT), lambda bi, t: (bi, 0, t)),
                   pl.BlockSpec((None, TT, LANES), tile)],
        out_shape=[jax.ShapeDtypeStruct((b, s, d), F32),
                   jax.ShapeDtypeStruct((b, s, d), BF16),
                   jax.ShapeDtypeStruct((b, N_EXPERTS, s), F32),
                   jax.ShapeDtypeStruct((b, s, LANES), BF16)],
        compiler_params=_cparams(2),
        name="mix_out",
    )(a, bmix, u, u, u, x, modsel, cw, cb, lg, lb, wo, g2, wr_hi, wr_lo)


def _route_kernel(aff_ref, tri_ref, pos_ref, post_ref, start_ref, cnt_ref, *, segments):
    bi = pl.program_id(0)
    tri = tri_ref[...]
    ne = N_EXPERTS
    lane_t = lax.broadcasted_iota(I32, (ne, LANES), 1)
    starts = jnp.zeros((ne, LANES), F32)
    cnts = jnp.zeros((ne, LANES), F32)

    def block_cumsum(mask_f32, carry):
        inc = _dot(mask_f32.astype(BF16), tri) + carry
        return inc - mask_f32, inc[:, TT - 1:TT]

    for (tile0, ntiles, cap, row0, rows_per_sample) in segments:
        lo = tile0 * TT
        n = ntiles * TT
        bits = pltpu.bitcast(aff_ref[:, lo:lo + n], I32)

        def search(i, cur):
            cand = cur | lax.shift_left(jnp.int32(1), 30 - i)
            c = jnp.sum(jnp.where(bits >= cand, 1.0, 0.0), axis=1, keepdims=True)
            return jnp.where(c >= cap, cand, cur)

        thr = lax.fori_loop(0, 31, search, jnp.zeros((ne, 1), I32))
        n_gt = jnp.sum(jnp.where(bits > thr, 1.0, 0.0), axis=1, keepdims=True)
        need = cap - n_gt
        base = (row0 + bi * rows_per_sample).astype(F32)
        tie_carry = jnp.zeros((ne, 1), F32)
        pos_carry = jnp.zeros((ne, 1), F32)
        for k in range(ntiles):
            bk = bits[:, k * TT:(k + 1) * TT]
            eq = jnp.where(bk == thr, 1.0, 0.0)
            tie_rank, tie_carry = block_cumsum(eq, tie_carry)
            sel = jnp.where(bk > thr, 1.0, jnp.where(tie_rank < need, eq, 0.0))
            rank, new_carry = block_cumsum(sel, pos_carry)
            pos = jnp.where(sel > 0.0, rank + base, -1.0)
            pos_ref[:, lo + k * TT:lo + (k + 1) * TT] = pos.astype(I32)
            padded = jnp.concatenate([pos, jnp.full((LANES - ne, TT), -1.0, F32)], axis=0)
            post_ref[lo + k * TT:lo + (k + 1) * TT, :] = padded.T.astype(I32)
            starts = jnp.where(lane_t == tile0 + k, pos_carry + base, starts)
            cnts = jnp.where(lane_t == tile0 + k, new_carry - pos_carry, cnts)
            pos_carry = new_carry
    start_ref[...] = starts.astype(I32)
    cnt_ref[...] = cnts.astype(I32)


def _route(aff_t, tri, segments):
    b, ne, s = aff_t.shape
    return pl.pallas_call(
        functools.partial(_route_kernel, segments=segments),
        grid=(b,),
        in_specs=[pl.BlockSpec((None, ne, s), lambda bi: (bi, 0, 0)),
                  pl.BlockSpec((TT, TT), lambda bi: (0, 0))],
        out_specs=[pl.BlockSpec((None, ne, s), lambda bi: (bi, 0, 0)),
                   pl.BlockSpec((None, s, LANES), lambda bi: (bi, 0, 0)),
                   pl.BlockSpec((None, ne, LANES), lambda bi: (bi, 0, 0)),
                   pl.BlockSpec((None, ne, LANES), lambda bi: (bi, 0, 0))],
        out_shape=[jax.ShapeDtypeStruct((b, ne, s), I32),
                   jax.ShapeDtypeStruct((b, s, LANES), I32),
                   jax.ShapeDtypeStruct((b, ne, LANES), I32),
                   jax.ShapeDtypeStruct((b, ne, LANES), I32)],
        compiler_params=_cparams(1),
        name="route",
    )(aff_t, tri)


def _step_to_tile(step, n_lat_steps, lat_tiles):
    is_lat = step < n_lat_steps
    bi = jnp.where(is_lat, step // lat_tiles, step - n_lat_steps)
    t = jnp.where(is_lat, 1 + step % lat_tiles, 0)
    return bi, t


def _dispatch_kernel(start_sm, cnt_sm, h_ref, pos_ref, gate_ref, xe_ref, xbuf, carry, sems,
                     *, n_lat_steps, lat_tiles, nt, d, slot_rows):
    step = pl.program_id(0)
    bi, t = _step_to_tile(step, n_lat_steps, lat_tiles)
    ne = N_EXPERTS
    rnd = DISPATCH_ROUND
    sub = SUBLANES

    @pl.when(step == 0)
    def _():
        carry[...] = jnp.zeros_like(carry)
        xbuf[0:rnd, :] = jnp.zeros((rnd, xbuf.shape[1]), F32)
        pads = [pltpu.make_async_copy(xbuf.at[pl.ds(0, rnd)], xe_ref.at[e, pl.ds(slot_rows, rnd)], sems.at[e])
                for e in range(ne)]
        for cp in pads:
            cp.start()
        for cp in pads:
            cp.wait()

    sbase = (bi * nt + t) * ne
    starts = [start_sm[sbase + e] for e in range(ne)]
    cnts = [cnt_sm[sbase + e] for e in range(ne)]
    ends = [starts[e] + cnts[e] for e in range(ne)]
    a0 = [(starts[e] // sub) * sub for e in range(ne)]
    last_grp = [(ends[e] // sub) * sub for e in range(ne)]
    n_rounds = jnp.int32(0)
    for e in range(ne):
        n_rounds = jnp.maximum(n_rounds, jnp.where(cnts[e] > 0, (ends[e] - a0[e] + rnd - 1) // rnd, 0))
    h = h_ref[...]
    gates = gate_ref[...]
    pos = pos_ref[...]
    row = lax.broadcasted_iota(I32, (rnd, TT), 0)
    grp_row = lax.broadcasted_iota(I32, (sub, xbuf.shape[1]), 0)

    def copy(e, r):
        dst = pl.ds(pl.multiple_of(a0[e] + r * rnd, sub), rnd)
        return pltpu.make_async_copy(xbuf.at[pl.ds(e * rnd, rnd)], xe_ref.at[e, dst], sems.at[e])

    def one_round(r):
        sel = jnp.concatenate(
            [jnp.where(pos[e:e + 1, :] - (a0[e] + r * rnd) == row, 1.0, 0.0).astype(BF16)
             for e in range(ne)], axis=0)
        xbuf[:, 0:d] = _dot(sel, h)
        xbuf[:, d:] = _dot(sel, gates)
        if r == 0:
            for e in range(ne):
                xbuf[e * rnd:e * rnd + sub, :] += carry[e]

        for e in range(ne):
            active = jnp.logical_and(cnts[e] > 0, a0[e] + r * rnd < ends[e])
            partial = jnp.logical_and(ends[e] > last_grp[e], (last_grp[e] - a0[e]) // rnd == r)

            @pl.when(jnp.logical_and(cnts[e] > 0, partial))
            def _():
                off = pl.multiple_of(e * rnd + (last_grp[e] - a0[e]) - r * rnd, sub)
                carry[e] = jnp.where(grp_row < ends[e] - last_grp[e], xbuf[pl.ds(off, sub), :], 0.0)

            @pl.when(active)
            def _():
                copy(e, r).start()

        for e in range(ne):
            @pl.when(jnp.logical_and(cnts[e] > 0, a0[e] + r * rnd < ends[e]))
            def _():
                copy(e, r).wait()

    for r in range((TT + sub - 1 + rnd - 1) // rnd):
        pl.when(r < n_rounds)(functools.partial(one_round, r))
    for e in range(ne):
        @pl.when(jnp.logical_and(cnts[e] > 0, ends[e] == last_grp[e]))
        def _():
            carry[e] = jnp.zeros((sub, xbuf.shape[1]), F32)


def _dispatch(starts, cnts, h2, pos, gate3, slot_rows):
    b, s, d = h2.shape
    nt = s // TT
    lat_tiles = nt - 1
    n_lat_steps = b * lat_tiles
    steps = n_lat_steps + b
    decode = functools.partial(_step_to_tile, n_lat_steps=n_lat_steps, lat_tiles=lat_tiles)

    def tile(step, *_):
        bi, t = decode(step)
        return bi, t, 0

    def pos_tile(step, *_):
        bi, t = decode(step)
        return bi, 0, t

    rows = N_EXPERTS * DISPATCH_ROUND
    width = d + LANES
    return pl.pallas_call(
        functools.partial(_dispatch_kernel, n_lat_steps=n_lat_steps, lat_tiles=lat_tiles, nt=nt, d=d,
                          slot_rows=slot_rows),
        grid_spec=pltpu.PrefetchScalarGridSpec(
            num_scalar_prefetch=2,
            grid=(steps,),
            in_specs=[pl.BlockSpec((None, TT, d), tile),
                      pl.BlockSpec((None, N_EXPERTS, TT), pos_tile),
                      pl.BlockSpec((None, TT, LANES), tile)],
            out_specs=pl.BlockSpec(memory_space=pl.ANY),
            scratch_shapes=[pltpu.VMEM((rows, width), F32),
                            pltpu.VMEM((N_EXPERTS, SUBLANES, width), F32),
                            pltpu.SemaphoreType.DMA((N_EXPERTS,))]),
        out_shape=jax.ShapeDtypeStruct((N_EXPERTS, slot_rows + DISPATCH_ROUND, width), F32),
        compiler_params=_cparams(1),
        name="dispatch",
    )(starts, cnts, h2, pos, gate3)


def _ffn_kernel(x_ref, wg_ref, wu_ref, wd_ref, y_ref, *, d):
    e = pl.program_id(0)
    x = x_ref[:, 0:d].astype(BF16)
    gt = _dot(x, wg_ref[...])
    up = _dot(x, wu_ref[...])
    hid = (gt * jax.nn.sigmoid(gt) * up).astype(BF16)
    y = _dot(hid, wd_ref[...])
    g3 = x_ref[:, d:]
    lane = lax.broadcasted_iota(I32, g3.shape, 1)
    mine = jnp.logical_and(lane % N_EXPERTS == e, lane < 3 * N_EXPERTS)
    gate = jnp.sum(jnp.where(mine, g3, 0.0), axis=1, keepdims=True)
    y_ref[...] = (y * gate).astype(y_ref.dtype)


def _ffn_tile(rows):
    best = BF16_ROWS
    for m in range(BF16_ROWS, 641, BF16_ROWS):
        if rows % m == 0:
            best = m
    return best


def _ffn(xe, wg, wu, wd, slot_rows):
    ne, _, width = xe.shape
    d, hdim = wg.shape[-2:]
    mt = _ffn_tile(slot_rows)
    wspec = lambda shape: pl.BlockSpec((None,) + shape, lambda e, j: (e, 0, 0))
    return pl.pallas_call(
        functools.partial(_ffn_kernel, d=d),
        grid=(ne, slot_rows // mt),
        in_specs=[pl.BlockSpec((None, mt, width), lambda e, j: (e, j, 0)),
                  wspec((d, hdim)), wspec((d, hdim)), wspec((hdim, d))],
        out_specs=pl.BlockSpec((None, mt, d), lambda e, j: (e, j, 0)),
        out_shape=jax.ShapeDtypeStruct((ne, slot_rows, d), BF16),
        compiler_params=_cparams(2),
        name="ffn",
    )(xe, wg, wu, wd)


def _combine_kernel(start_sm, cnt_sm, x_ref, post_ref, mod_ref, fg_ref, y_ref, o_ref, ybuf, acc_ref, sems,
                    *, d, nt, slot_rows, final):
    bi = pl.program_id(0)
    t = pl.program_id(1) + (1 if final else 0)
    ne = N_EXPERTS
    win = COMBINE_WINDOW
    sbase = (bi * nt + t) * ne
    starts = [start_sm[sbase + e] for e in range(ne)]
    cmax = cnt_sm[sbase]
    for e in range(1, ne):
        cmax = jnp.maximum(cmax, cnt_sm[sbase + e])
    post = post_ref[...]
    lane = lax.broadcasted_iota(I32, (TT, win), 1)

    def one_round(r):
        a = [jnp.minimum((starts[e] // BF16_ROWS) * BF16_ROWS + r * COMBINE_ROUND, slot_rows - win)
             for e in range(ne)]
        cps = [pltpu.make_async_copy(y_ref.at[e, pl.ds(pl.multiple_of(a[e], BF16_ROWS), win)],
                                     ybuf.at[pl.ds(e * win, win)], sems.at[e]) for e in range(ne)]
        for cp in cps:
            cp.start()
        sel = []
        for e in range(ne):
            rel = post[:, e:e + 1] - starts[e]
            in_round = jnp.logical_and(rel >= r * COMBINE_ROUND, rel < (r + 1) * COMBINE_ROUND)
            hit = jnp.logical_and(in_round, post[:, e:e + 1] - a[e] == lane)
            sel.append(jnp.where(hit, 1.0, 0.0).astype(BF16))
        sel = jnp.concatenate(sel, axis=1)
        for cp in cps:
            cp.wait()
        acc_ref[...] += _dot(sel, ybuf[...])

    n_rounds = (cmax + COMBINE_ROUND - 1) // COMBINE_ROUND
    acc_ref[...] = jnp.zeros_like(acc_ref)
    for r in range((TT + COMBINE_ROUND - 1) // COMBINE_ROUND):
        pl.when(r < n_rounds)(functools.partial(one_round, r))
    x2 = x_ref[...] + mod_ref[:, 5 * d:6 * d] * acc_ref[...]
    if final:
        x2 = x2 * lax.rsqrt(jnp.mean(x2 * x2, axis=-1, keepdims=True) + EPS) * fg_ref[...]
    o_ref[...] = x2


def _combine(starts, cnts, x1, pos_t, modsel, final_g, y, slot_rows, final):
    b, s, d = x1.shape
    nt = s // TT
    t0 = 1 if final else 0
    tile = lambda bi, t, *_: (bi, t + t0, 0)
    out_rows = s - t0 * TT if final else s
    out_tile = (lambda bi, t, *_: (bi, t, 0)) if final else tile
    return pl.pallas_call(
        functools.partial(_combine_kernel, d=d, nt=nt, slot_rows=slot_rows, final=final),
        grid_spec=pltpu.PrefetchScalarGridSpec(
            num_scalar_prefetch=2,
            grid=(b, nt - t0),
            in_specs=[pl.BlockSpec((None, TT, d), tile),
                      pl.BlockSpec((None, TT, LANES), tile),
                      pl.BlockSpec((None, None, 1, 6 * d), lambda bi, t, *_: (bi, jnp.minimum(t + t0, 1), 0, 0)),
                      pl.BlockSpec((1, d), lambda bi, t, *_: (0, 0)),
                      pl.BlockSpec(memory_space=pl.ANY)],
            out_specs=pl.BlockSpec((None, TT, d), out_tile),
            scratch_shapes=[pltpu.VMEM((N_EXPERTS * COMBINE_WINDOW, d), BF16),
                            pltpu.VMEM((TT, d), F32),
                            pltpu.SemaphoreType.DMA((N_EXPERTS,))]),
        out_shape=jax.ShapeDtypeStruct((b, out_rows, d), F32),
        compiler_params=_cparams(2),
        name="combine",
    )(starts, cnts, x1, pos_t, modsel, final_g, y)


def _rope_tables(n, ctx):
    tkn = np.arange(n)
    half = HEAD_DIM // 4
    inv = ROPE_THETA ** (-np.arange(half, dtype=np.float64) / half)
    ang_row = (tkn // GRID_W)[:, None] * inv[None, :]
    ang_col = (tkn % GRID_W)[:, None] * inv[None, :]
    ang = np.concatenate([ang_row, ang_row, ang_col, ang_col], axis=1)
    sign = np.tile(np.concatenate([-np.ones(half), np.ones(half)]), 2)
    cos = np.concatenate([np.ones((ctx, HEAD_DIM)), np.cos(ang)], axis=0)
    sin = np.concatenate([np.zeros((ctx, HEAD_DIM)), np.sin(ang) * sign[None, :]], axis=0)
    reps = LANES // HEAD_DIM
    return (jnp.asarray(np.tile(cos, (1, reps)), F32), jnp.asarray(np.tile(sin, (1, reps)), F32))


def _block_diag_ones(width):
    idx = np.arange(width) // HEAD_DIM
    return jnp.asarray(idx[:, None] == idx[None, :], BF16)


def kernel(x, c, ctx, c_ctx, w_ada, b_ada, norm1_g, norm2_g, w_in, q_norm_g, k_norm_g, na_rpb, conv_w, conv_b,
           conv_ln_g, conv_ln_b, w_out, w_router, w_gate, w_up, w_down, final_norm_g):
    b, n, d = x.shape
    nctx = ctx.shape[1]
    depth = w_ada.shape[0]
    assert nctx == TT and n % TT == 0 and n % GRID_W == 0 and b <= SUBLANES - 1
    assert w_router.shape[-1] == N_EXPERTS and w_in.shape[-1] == IN_WIDTH
    s = nctx + n
    nt = s // TT
    cap_lat = CAPACITY_FACTOR * n // N_EXPERTS
    cap_ctx = CAPACITY_FACTOR * nctx // N_EXPERTS
    assert cap_ctx <= DISPATCH_ROUND and cap_lat % BF16_ROWS == 0
    lat_rows = b * cap_lat
    ctx_rows = b * DISPATCH_ROUND

    cond = jnp.zeros((SUBLANES, d), F32).at[:b].set(c).at[b].set(c_ctx)
    mods = _ada(cond, w_ada, b_ada)
    cos, sin = _rope_tables(n, nctx)
    ones_q = _block_diag_ones(A_WIDTH)
    ones_k = _block_diag_ones(A_KV_WIDTH)
    tri = jnp.asarray(np.triu(np.ones((TT, TT))), BF16)

    stream = jnp.concatenate([ctx, x], axis=1)
    slot_rows = lat_rows + ctx_rows
    segments = ((0, 1, cap_ctx, lat_rows, DISPATCH_ROUND), (1, nt - 1, cap_lat, 0, cap_lat))
    bias_rows = n // GRID_W
    for i in range(depth):
        m = mods[i]
        modsel = jnp.stack([jnp.broadcast_to(m[b], (b, 6 * d)), m[:b]], axis=1)[:, :, None, :]
        qg = jnp.tile(q_norm_g[i], A_HEADS)[None, :]
        kg = jnp.tile(k_norm_g[i], A_KV_HEADS)[None, :]
        qa, ka, va, qb, kb, vb, u = _in_proj(stream, modsel, norm1_g[i][None, :], w_in[i].astype(BF16),
                                             cos, sin, qg, kg, ones_q, ones_k)
        a_mix = _attn_a(qa, ka, va, nctx)
        b_mix = _attn_b(qb, kb, vb, _na_bias_tables(na_rpb[i], bias_rows), nctx)
        wr = jnp.zeros((d, LANES), F32).at[:, :N_EXPERTS].set(w_router[i])
        wr_hi = wr.astype(BF16)
        wr_lo = (wr - wr_hi.astype(F32)).astype(BF16)
        x1, h2, aff_t, gate3 = _mix_out(a_mix, b_mix, u, stream, modsel, conv_w[i], conv_b[i][None, :],
                                        conv_ln_g[i][None, :], conv_ln_b[i][None, :], w_out[i].astype(BF16),
                                        norm2_g[i][None, :], wr_hi, wr_lo)
        pos, pos_t, start_t, cnt_t = _route(aff_t, tri, segments)
        starts = jnp.transpose(start_t, (0, 2, 1))[:, :nt, :].reshape(-1)
        cnts = jnp.transpose(cnt_t, (0, 2, 1))[:, :nt, :].reshape(-1)
        xe = _dispatch(starts, cnts, h2, pos, gate3, slot_rows)
        y = _ffn(xe, w_gate[i].astype(BF16), w_up[i].astype(BF16), w_down[i].astype(BF16), slot_rows)
        stream = _combine(starts, cnts, x1, pos_t, modsel, final_norm_g[None, :], y, slot_rows, i == depth - 1)
    return stream
```

```python
import functools

import numpy as np
import jax
import jax.numpy as jnp
from jax import lax
from jax.experimental import pallas as pl
from jax.experimental.pallas import tpu as pltpu

F32 = jnp.float32
BF16 = jnp.bfloat16
I32 = jnp.int32

GRID_W = 64
HEAD_DIM = 64
A_HEADS = 8
A_KV_HEADS = 2
A_GROUP = A_HEADS // A_KV_HEADS
B_HEADS = 4
C_CHANNELS = 256
CONV_WIDTH = 31
NA_WIN_ROWS = 8
NA_WIN_COLS = 16
N_EXPERTS = 16
CAPACITY_FACTOR = 2
ROPE_THETA = 10000.0
EPS = 1e-6
A_WIDTH = A_HEADS * HEAD_DIM
A_KV_WIDTH = A_KV_HEADS * HEAD_DIM
B_WIDTH = B_HEADS * HEAD_DIM
MIX_WIDTH = A_WIDTH + B_WIDTH + C_CHANNELS
IN_WIDTH = A_WIDTH + 2 * A_KV_WIDTH + 3 * B_WIDTH + 2 * C_CHANNELS

LANES = 128
SUBLANES = 8
BF16_ROWS = 16
VMEM_LIMIT = 56 * 1024 * 1024

TT = 256
NEG = -1e30
LOG2_E = float(np.log2(np.e))
CONV_HALO = 16
DISPATCH_ROUND = 64
COMBINE_WINDOW = 128
COMBINE_ROUND = COMBINE_WINDOW - BF16_ROWS


def _cparams(n_axes):
    return pltpu.CompilerParams(dimension_semantics=("arbitrary",) * n_axes, vmem_limit_bytes=VMEM_LIMIT)


def _split3(x):
    hi = x.astype(BF16)
    r1 = x - hi.astype(F32)
    mid = r1.astype(BF16)
    lo = (r1 - mid.astype(F32)).astype(BF16)
    return hi, mid, lo


def _dot(a, b):
    return jnp.dot(a, b, preferred_element_type=F32)


def _dot_nt(a, b):
    return lax.dot_general(a, b, (((1,), (1,)), ((), ())), preferred_element_type=F32)


def _dot3(a, b_hi, b_lo):
    a_hi = a.astype(BF16)
    a_lo = (a - a_hi.astype(F32)).astype(BF16)
    return _dot(a_hi, b_hi) + (_dot(a_lo, b_hi) + _dot(a_hi, b_lo))


def _ada_kernel(cond_ref, w_ref, b_ref, o_ref):
    c = cond_ref[...]
    s = c * jax.nn.sigmoid(c)
    w = w_ref[0]
    w_hi = w.astype(BF16)
    w_lo = (w - w_hi.astype(F32)).astype(BF16)
    o_ref[0] = _dot3(s, w_hi, w_lo) + b_ref[0]


def _ada(cond, w_ada, b_ada):
    depth, d, six_d = w_ada.shape
    bn = six_d // 4
    return pl.pallas_call(
        _ada_kernel,
        grid=(depth, six_d // bn),
        in_specs=[pl.BlockSpec((SUBLANES, d), lambda l, j: (0, 0)),
                  pl.BlockSpec((1, d, bn), lambda l, j: (l, 0, j)),
                  pl.BlockSpec((1, 1, bn), lambda l, j: (l, 0, j))],
        out_specs=pl.BlockSpec((1, SUBLANES, bn), lambda l, j: (l, 0, j)),
        out_shape=jax.ShapeDtypeStruct((depth, SUBLANES, six_d), F32),
        compiler_params=_cparams(2),
        name="ada",
    )(cond, w_ada, b_ada.reshape(depth, 1, six_d))


def _head_rms(x, ones_bd, g):
    sq = x * x
    hi = sq.astype(BF16)
    lo = (sq - hi.astype(F32)).astype(BF16)
    ssum = _dot(hi, ones_bd) + _dot(lo, ones_bd)
    return x * lax.rsqrt(ssum * (1.0 / HEAD_DIM) + EPS) * g


def _rope(x, cos, sin_signed):
    w = x.shape[-1]
    lane = lax.broadcasted_iota(I32, x.shape, 1)
    first_half = (lane % 32) < 16
    partner = jnp.where(first_half, pltpu.roll(x, w - 16, 1), pltpu.roll(x, 16, 1))
    return x * cos + partner * sin_signed


def _in_proj_kernel(x_ref, mod_ref, g1_ref, w_ref, cos_ref, sin_ref, qg_ref, kg_ref, onesq_ref, onesk_ref,
                    qa_ref, ka_ref, va_ref, qb_ref, kb_ref, vb_ref, u_ref, *, d):
    x = x_ref[...]
    y = x * lax.rsqrt(jnp.mean(x * x, axis=-1, keepdims=True) + EPS) * g1_ref[...]
    shift = mod_ref[:, 0:d]
    scale = mod_ref[:, d:2 * d]
    h = (y * (1.0 + scale) + shift).astype(BF16)
    p = _dot(h, w_ref[...])
    o = 0
    qa = p[:, o:o + A_WIDTH]; o += A_WIDTH
    ka = p[:, o:o + A_KV_WIDTH]; o += A_KV_WIDTH
    va = p[:, o:o + A_KV_WIDTH]; o += A_KV_WIDTH
    qb = p[:, o:o + B_WIDTH]; o += B_WIDTH
    kb = p[:, o:o + B_WIDTH]; o += B_WIDTH
    vb = p[:, o:o + B_WIDTH]; o += B_WIDTH
    cv = p[:, o:o + C_CHANNELS]; o += C_CHANNELS
    cg = p[:, o:o + C_CHANNELS]

    cos = cos_ref[...]
    sin = sin_ref[...]
    sm_scale = HEAD_DIM ** -0.5
    qa = _head_rms(qa, onesq_ref[...], qg_ref[...])
    qa = _rope(qa, jnp.concatenate([cos] * (A_WIDTH // LANES), axis=1),
               jnp.concatenate([sin] * (A_WIDTH // LANES), axis=1)) * (sm_scale * LOG2_E)
    ka = _head_rms(ka, onesk_ref[...], kg_ref[...])
    ka = _rope(ka, cos, sin)
    for hh in range(A_HEADS):
        qa_ref[hh] = qa[:, hh * HEAD_DIM:(hh + 1) * HEAD_DIM].astype(BF16)
    for hh in range(A_KV_HEADS):
        ka_ref[hh] = ka[:, hh * HEAD_DIM:(hh + 1) * HEAD_DIM].astype(BF16)
        va_ref[hh] = va[:, hh * HEAD_DIM:(hh + 1) * HEAD_DIM].astype(BF16)
    qb_ref[...] = (qb * sm_scale).astype(BF16)
    kb_ref[...] = kb.astype(BF16)
    vb_ref[...] = vb.astype(BF16)
    u_ref[...] = cv * jax.nn.sigmoid(cg)


def _in_proj(x, modsel, g1, w_in, cos, sin, qg, kg, ones_q, ones_k):
    b, s, d = x.shape
    nt = s // TT
    tile = lambda bi, t: (bi, t, 0)
    head_tile = lambda bi, t: (bi, 0, t, 0)
    const2 = lambda bi, t: (0, 0)
    return pl.pallas_call(
        functools.partial(_in_proj_kernel, d=d),
        grid=(b, nt),
        in_specs=[pl.BlockSpec((None, TT, d), tile),
                  pl.BlockSpec((None, None, 1, 6 * d), lambda bi, t: (bi, jnp.minimum(t, 1), 0, 0)),
                  pl.BlockSpec((1, d), const2),
                  pl.BlockSpec((d, IN_WIDTH), const2),
                  pl.BlockSpec((TT, LANES), lambda bi, t: (t, 0)),
                  pl.BlockSpec((TT, LANES), lambda bi, t: (t, 0)),
                  pl.BlockSpec((1, A_WIDTH), const2),
                  pl.BlockSpec((1, A_KV_WIDTH), const2),
                  pl.BlockSpec((A_WIDTH, A_WIDTH), const2),
                  pl.BlockSpec((A_KV_WIDTH, A_KV_WIDTH), const2)],
        out_specs=[pl.BlockSpec((None, A_HEADS, TT, HEAD_DIM), head_tile),
                   pl.BlockSpec((None, A_KV_HEADS, TT, HEAD_DIM), head_tile),
                   pl.BlockSpec((None, A_KV_HEADS, TT, HEAD_DIM), head_tile),
                   pl.BlockSpec((None, TT, B_WIDTH), tile),
                   pl.BlockSpec((None, TT, B_WIDTH), tile),
                   pl.BlockSpec((None, TT, B_WIDTH), tile),
                   pl.BlockSpec((None, TT, C_CHANNELS), tile)],
        out_shape=[jax.ShapeDtypeStruct((b, A_HEADS, s, HEAD_DIM), BF16),
                   jax.ShapeDtypeStruct((b, A_KV_HEADS, s, HEAD_DIM), BF16),
                   jax.ShapeDtypeStruct((b, A_KV_HEADS, s, HEAD_DIM), BF16),
                   jax.ShapeDtypeStruct((b, s, B_WIDTH), BF16),
                   jax.ShapeDtypeStruct((b, s, B_WIDTH), BF16),
                   jax.ShapeDtypeStruct((b, s, B_WIDTH), BF16),
                   jax.ShapeDtypeStruct((b, s, C_CHANNELS), F32)],
        compiler_params=_cparams(2),
        name="in_proj",
    )(x, modsel, g1, w_in, cos, sin, qg, kg, ones_q, ones_k)


def _attn_a_kernel(q_ref, k_ref, v_ref, o_ref, *, ctx, tk, n_chunks):
    t = pl.program_id(2)
    g, tq, dh = q_ref.shape
    qs = [q_ref[i] for i in range(g)]

    def update(state, s, v):
        m, l, acc = state
        m_new = jnp.maximum(m, jnp.max(s, axis=1, keepdims=True))
        alpha = jnp.exp2(m - m_new)
        p = jnp.exp2(s - m_new)
        l = alpha * l + jnp.sum(p, axis=1, keepdims=True)
        return m_new, l, alpha * acc + _dot(p.astype(BF16), v)

    def step(s0, k, v, k_next, states):
        out = []
        s = s0
        for i in range(g):
            s_next = _dot_nt(qs[i + 1], k) if i + 1 < g else _dot_nt(qs[0], k_next)
            out.append(update(states[i], s, v))
            s = s_next
        return s, tuple(out)

    def chunk(c):
        start = pl.multiple_of(ctx + c * tk, int(np.gcd(ctx, tk)))
        return pl.ds(start, tk)

    init = tuple((jnp.full((tq, 1), NEG, F32), jnp.zeros((tq, 1), F32), jnp.zeros((tq, dh), F32))
                 for _ in range(g))
    s0 = _dot_nt(qs[0], k_ref[0:ctx, :])

    def finish(states):
        o_ref[...] = jnp.concatenate([acc / l for (_, l, acc) in states], axis=1).astype(o_ref.dtype)

    @pl.when(t == 0)
    def _():
        _, states = step(s0, k_ref[0:ctx, :], v_ref[0:ctx, :], k_ref[0:ctx, :], init)
        finish(states)

    @pl.when(t > 0)
    def _():
        carry = step(s0, k_ref[0:ctx, :], v_ref[0:ctx, :], k_ref[chunk(0), :], init)

        def body(c, carry):
            s_head0, states = carry
            nxt = jnp.minimum(c + 1, n_chunks - 1)
            return step(s_head0, k_ref[chunk(c), :], v_ref[chunk(c), :], k_ref[chunk(nxt), :], states)

        finish(lax.fori_loop(0, n_chunks, body, carry, unroll=True)[1])


def _attn_a(qa, ka, va, ctx):
    b, _, s, dh = qa.shape
    nt = s // TT
    tk = 1024 if (s - ctx) % 1024 == 0 else TT
    return pl.pallas_call(
        functools.partial(_attn_a_kernel, ctx=ctx, tk=tk, n_chunks=(s - ctx) // tk),
        grid=(b, A_KV_HEADS, nt),
        in_specs=[pl.BlockSpec((None, A_GROUP, TT, dh), lambda bi, j, t: (bi, j, t, 0)),
                  pl.BlockSpec((None, None, s, dh), lambda bi, j, t: (bi, j, 0, 0)),
                  pl.BlockSpec((None, None, s, dh), lambda bi, j, t: (bi, j, 0, 0))],
        out_specs=pl.BlockSpec((None, TT, A_GROUP * dh), lambda bi, j, t: (bi, t, j)),
        out_shape=jax.ShapeDtypeStruct((b, s, A_WIDTH), BF16),
        compiler_params=_cparams(3),
        name="attn_a",
    )(qa, ka, va)


def _attn_b_kernel(q_ref, k_ref, v_ref, bias_ref, o_ref, *, ctx, rows, wr):
    t = pl.program_id(1)
    tt, w = q_ref.shape
    rows_per_tile = tt // GRID_W
    win = wr * GRID_W
    q = q_ref[...]
    kc = k_ref[0:ctx, :]
    vc = v_ref[0:ctx, :]
    lane_head = lax.broadcasted_iota(I32, (tt, w), 1) // HEAD_DIM

    def softmax_parts(parts):
        m = parts[0].max(axis=1, keepdims=True)
        for s in parts[1:]:
            m = jnp.maximum(m, s.max(axis=1, keepdims=True))
        es = [jnp.exp(s - m) for s in parts]
        tot = es[0].sum(axis=1, keepdims=True)
        for e in es[1:]:
            tot = tot + e.sum(axis=1, keepdims=True)
        inv = 1.0 / tot
        return [(e * inv).astype(BF16) for e in es]

    @pl.when(t == 0)
    def _():
        out = jnp.zeros((tt, w), F32)
        for hh in range(B_HEADS):
            qh = jnp.where(lane_head == hh, q, jnp.zeros_like(q))
            (p,) = softmax_parts([_dot_nt(qh, kc)])
            out = out + jnp.where(lane_head == hh, _dot(p, vc), 0.0)
        o_ref[...] = out.astype(o_ref.dtype)

    @pl.when(t > 0)
    def _():
        out = jnp.zeros((tt, w), F32)
        r0 = (t - 1) * rows_per_tile
        for hh in range(B_HEADS):
            qh = jnp.where(lane_head == hh, q, jnp.zeros_like(q))
            s_cx = _dot_nt(qh, kc)
            s_nb = []
            v_rows = []
            for i in range(rows_per_tile):
                r = r0 + i
                rs = jnp.clip(r - wr // 2, 0, rows - wr)
                start = pl.multiple_of(ctx + rs * GRID_W, GRID_W)
                k_rows = k_ref[pl.ds(start, win), :]
                v_rows.append(v_ref[pl.ds(start, win), :])
                s_nb.append(_dot_nt(qh[i * GRID_W:(i + 1) * GRID_W], k_rows) + bias_ref[r - rs, hh])
            p_cx, p_nb = softmax_parts([s_cx, jnp.concatenate(s_nb, axis=0)])
            o_cx = _dot(p_cx, vc)
            o_nb = jnp.concatenate([_dot(p_nb[i * GRID_W:(i + 1) * GRID_W], v_rows[i])
                                    for i in range(rows_per_tile)], axis=0)
            out = out + jnp.where(lane_head == hh, o_cx + o_nb, 0.0)
        o_ref[...] = out.astype(o_ref.dtype)


def _attn_b(qb, kb, vb, bias, ctx):
    b, s, w = qb.shape
    nt = s // TT
    rows = (s - ctx) // GRID_W
    wr = min(NA_WIN_ROWS, rows)
    tile = lambda bi, t: (bi, t, 0)
    whole = lambda bi, t: (bi, 0, 0)
    return pl.pallas_call(
        functools.partial(_attn_b_kernel, ctx=ctx, rows=rows, wr=wr),
        grid=(b, nt),
        in_specs=[pl.BlockSpec((None, TT, w), tile),
                  pl.BlockSpec((None, s, w), whole),
                  pl.BlockSpec((None, s, w), whole),
                  pl.BlockSpec(bias.shape, lambda bi, t: (0, 0, 0, 0))],
        out_specs=pl.BlockSpec((None, TT, w), tile),
        out_shape=jax.ShapeDtypeStruct((b, s, w), BF16),
        compiler_params=_cparams(2),
        name="attn_b",
    )(qb, kb, vb, bias)


def _na_bias_tables(rpb, rows):
    wr = min(NA_WIN_ROWS, rows)
    wc = NA_WIN_COLS
    q = np.arange(GRID_W)
    cs = np.clip(q - wc // 2, 0, GRID_W - wc)
    kc = np.arange(GRID_W)
    valid = (kc[None, :] >= cs[:, None]) & (kc[None, :] < cs[:, None] + wc)
    dc = np.clip(kc[None, :] - q[:, None] + (NA_WIN_COLS - 1), 0, 2 * NA_WIN_COLS - 2)
    v = np.arange(wr)
    wrow = np.arange(wr)
    dr = wrow[None, :] - v[:, None] + (NA_WIN_ROWS - 1)
    pick_r = jnp.asarray(dr[None, :, :] == np.arange(2 * NA_WIN_ROWS - 1)[:, None, None], F32)
    pick_c = jnp.asarray((dc[None, :, :] == np.arange(2 * NA_WIN_COLS - 1)[:, None, None])
                         & valid[None, :, :], F32)
    tab = jnp.einsum("hrc,rvw,cqk->vhqwk", rpb.astype(F32), pick_r, pick_c, precision=lax.Precision.HIGHEST)
    tab = tab + jnp.asarray(np.where(valid, 0.0, NEG), F32)[None, None, :, None, :]
    return tab.reshape(wr, B_HEADS, GRID_W, wr * GRID_W)


def _mix_out_kernel(a_ref, b_ref, up_ref, uc_ref, un_ref, x_ref, mod_ref, cw_ref, cb_ref, lg_ref, lb_ref,
                    wo_ref, g2_ref, wrh_ref, wrl_ref,
                    x1_ref, h2_ref, aff_ref, gate_ref, *, d, nt):
    t = pl.program_id(1)
    tt = uc_ref.shape[0]
    prev_ok = t >= 2
    next_ok = jnp.logical_and(t >= 1, t < nt - 1)
    prev = jnp.where(prev_ok, up_ref[tt - CONV_HALO:tt, :], 0.0)
    nxt = jnp.where(next_ok, un_ref[0:CONV_HALO, :], 0.0)
    win = jnp.concatenate([prev, uc_ref[...], nxt], axis=0)
    off = CONV_HALO - CONV_WIDTH // 2
    acc = jnp.zeros((tt, C_CHANNELS), F32) + cb_ref[...]
    for k in range(CONV_WIDTH):
        acc = acc + win[off + k:off + k + tt, :] * cw_ref[k:k + 1, :]
    mu = jnp.mean(acc, axis=-1, keepdims=True)
    var = jnp.mean(jnp.square(acc - mu), axis=-1, keepdims=True)
    yc = (acc - mu) * lax.rsqrt(var + EPS) * lg_ref[...] + lb_ref[...]
    c = (yc * jax.nn.sigmoid(yc)).astype(BF16)

    mix = (_dot(a_ref[...], wo_ref[0:A_WIDTH, :])
           + _dot(b_ref[...], wo_ref[A_WIDTH:A_WIDTH + B_WIDTH, :])
           + _dot(c, wo_ref[A_WIDTH + B_WIDTH:MIX_WIDTH, :]))
    g1 = mod_ref[:, 2 * d:3 * d]
    x1 = x_ref[...] + g1 * mix
    x1_ref[...] = x1
    y = x1 * lax.rsqrt(jnp.mean(x1 * x1, axis=-1, keepdims=True) + EPS) * g2_ref[...]
    h2 = y * (1.0 + mod_ref[:, 4 * d:5 * d]) + mod_ref[:, 3 * d:4 * d]
    h2_ref[...] = h2.astype(BF16)
    logits = _dot3(h2, wrh_ref[...], wrl_ref[...])
    lane = lax.broadcasted_iota(I32, logits.shape, 1)
    logits = jnp.where(lane < N_EXPERTS, logits, NEG)
    e = jnp.exp(logits - jnp.max(logits, axis=-1, keepdims=True))
    aff = e / jnp.sum(e, axis=-1, keepdims=True)
    aff_ref[...] = aff.T[0:N_EXPERTS, :]
    hi, mid, lo = _split3(aff)
    gate = jnp.where(lane < N_EXPERTS, hi.astype(F32),
                     jnp.where(lane < 2 * N_EXPERTS, pltpu.roll(mid.astype(F32), N_EXPERTS, 1),
                               jnp.where(lane < 3 * N_EXPERTS, pltpu.roll(lo.astype(F32), 2 * N_EXPERTS, 1), 0.0)))
    gate_ref[...] = gate.astype(BF16)


def _mix_out(a, bmix, u, x, modsel, cw, cb, lg, lb, wo, g2, wr_hi, wr_lo):
    b, s, d = x.shape
    nt = s // TT
    tile = lambda bi, t: (bi, t, 0)
    const2 = lambda bi, t: (0, 0)
    return pl.pallas_call(
        functools.partial(_mix_out_kernel, d=d, nt=nt),
        grid=(b, nt),
        in_specs=[pl.BlockSpec((None, TT, A_WIDTH), tile),
                  pl.BlockSpec((None, TT, B_WIDTH), tile),
                  pl.BlockSpec((None, TT, C_CHANNELS), lambda bi, t: (bi, jnp.maximum(t - 1, 0), 0)),
                  pl.BlockSpec((None, TT, C_CHANNELS), tile),
                  pl.BlockSpec((None, TT, C_CHANNELS), lambda bi, t: (bi, jnp.minimum(t + 1, nt - 1), 0)),
                  pl.BlockSpec((None, TT, d), tile),
                  pl.BlockSpec((None, None, 1, 6 * d), lambda bi, t: (bi, jnp.minimum(t, 1), 0, 0)),
                  pl.BlockSpec((CONV_WIDTH, C_CHANNELS), const2),
                  pl.BlockSpec((1, C_CHANNELS), const2),
                  pl.BlockSpec((1, C_CHANNELS), const2),
                  pl.BlockSpec((1, C_CHANNELS), const2),
                  pl.BlockSpec((MIX_WIDTH, d), const2),
                  pl.BlockSpec((1, d), const2),
                  pl.BlockSpec((d, LANES), const2),
                  pl.BlockSpec((d, LANES), const2)],
        out_specs=[pl.BlockSpec((None, TT, d), tile),
                   pl.BlockSpec((None, TT, d), tile),
                   pl.BlockSpec((None, N_EXPERTS, TT), lambda bi, t: (bi, 0, t)),
                   pl.BlockSpec((None, TT, LANES), tile)],
        out_shape=[jax.ShapeDtypeStruct((b, s, d), F32),
                   jax.ShapeDtypeStruct((b, s, d), BF16),
                   jax.ShapeDtypeStruct((b, N_EXPERTS, s), F32),
                   jax.ShapeDtypeStruct((b, s, LANES), BF16)],
        compiler_params=_cparams(2),
        name="mix_out",
    )(a, bmix, u, u, u, x, modsel, cw, cb, lg, lb, wo, g2, wr_hi, wr_lo)


def _route_kernel(aff_ref, tri_ref, pos_ref, post_ref, start_ref, cnt_ref, *, segments):
    bi = pl.program_id(0)
    tri = tri_ref[...]
    ne = N_EXPERTS
    lane_t = lax.broadcasted_iota(I32, (ne, LANES), 1)
    starts = jnp.zeros((ne, LANES), F32)
    cnts = jnp.zeros((ne, LANES), F32)

    def block_cumsum(mask_f32, carry):
        inc = _dot(mask_f32.astype(BF16), tri) + carry
        return inc - mask_f32, inc[:, TT - 1:TT]

    for (tile0, ntiles, cap, row0, rows_per_sample) in segments:
        lo = tile0 * TT
        n = ntiles * TT
        bits = pltpu.bitcast(aff_ref[:, lo:lo + n], I32)

        def search(i, cur):
            cand = cur | lax.shift_left(jnp.int32(1), 30 - i)
            c = jnp.sum(jnp.where(bits >= cand, 1.0, 0.0), axis=1, keepdims=True)
            return jnp.where(c >= cap, cand, cur)

        thr = lax.fori_loop(0, 31, search, jnp.zeros((ne, 1), I32))
        n_gt = jnp.sum(jnp.where(bits > thr, 1.0, 0.0), axis=1, keepdims=True)
        need = cap - n_gt
        base = (row0 + bi * rows_per_sample).astype(F32)
        tie_carry = jnp.zeros((ne, 1), F32)
        pos_carry = jnp.zeros((ne, 1), F32)
        for k in range(ntiles):
            bk = bits[:, k * TT:(k + 1) * TT]
            eq = jnp.where(bk == thr, 1.0, 0.0)
            tie_rank, tie_carry = block_cumsum(eq, tie_carry)
            sel = jnp.where(bk > thr, 1.0, jnp.where(tie_rank < need, eq, 0.0))
            rank, new_carry = block_cumsum(sel, pos_carry)
            pos = jnp.where(sel > 0.0, rank + base, -1.0)
            pos_ref[:, lo + k * TT:lo + (k + 1) * TT] = pos.astype(I32)
            padded = jnp.concatenate([pos, jnp.full((LANES - ne, TT), -1.0, F32)], axis=0)
            post_ref[lo + k * TT:lo + (k + 1) * TT, :] = padded.T.astype(I32)
            starts = jnp.where(lane_t == tile0 + k, pos_carry + base, starts)
            cnts = jnp.where(lane_t == tile0 + k, new_carry - pos_carry, cnts)
            pos_carry = new_carry
    start_ref[...] = starts.astype(I32)
    cnt_ref[...] = cnts.astype(I32)


def _route(aff_t, tri, segments):
    b, ne, s = aff_t.shape
    return pl.pallas_call(
        functools.partial(_route_kernel, segments=segments),
        grid=(b,),
        in_specs=[pl.BlockSpec((None, ne, s), lambda bi: (bi, 0, 0)),
                  pl.BlockSpec((TT, TT), lambda bi: (0, 0))],
        out_specs=[pl.BlockSpec((None, ne, s), lambda bi: (bi, 0, 0)),
                   pl.BlockSpec((None, s, LANES), lambda bi: (bi, 0, 0)),
                   pl.BlockSpec((None, ne, LANES), lambda bi: (bi, 0, 0)),
                   pl.BlockSpec((None, ne, LANES), lambda bi: (bi, 0, 0))],
        out_shape=[jax.ShapeDtypeStruct((b, ne, s), I32),
                   jax.ShapeDtypeStruct((b, s, LANES), I32),
                   jax.ShapeDtypeStruct((b, ne, LANES), I32),
                   jax.ShapeDtypeStruct((b, ne, LANES), I32)],
        compiler_params=_cparams(1),
        name="route",
    )(aff_t, tri)


def _step_to_tile(step, n_lat_steps, lat_tiles):
    is_lat = step < n_lat_steps
    bi = jnp.where(is_lat, step // lat_tiles, step - n_lat_steps)
    t = jnp.where(is_lat, 1 + step % lat_tiles, 0)
    return bi, t


def _dispatch_kernel(start_sm, cnt_sm, h_ref, pos_ref, gate_ref, xe_ref, xbuf, carry, sems,
                     *, n_lat_steps, lat_tiles, nt, d, slot_rows):
    step = pl.program_id(0)
    bi, t = _step_to_tile(step, n_lat_steps, lat_tiles)
    ne = N_EXPERTS
    rnd = DISPATCH_ROUND
    sub = SUBLANES

    @pl.when(step == 0)
    def _():
        carry[...] = jnp.zeros_like(carry)
        xbuf[0:rnd, :] = jnp.zeros((rnd, xbuf.shape[1]), F32)
        pads = [pltpu.make_async_copy(xbuf.at[pl.ds(0, rnd)], xe_ref.at[e, pl.ds(slot_rows, rnd)], sems.at[e])
                for e in range(ne)]
        for cp in pads:
            cp.start()
        for cp in pads:
            cp.wait()

    sbase = (bi * nt + t) * ne
    starts = [start_sm[sbase + e] for e in range(ne)]
    cnts = [cnt_sm[sbase + e] for e in range(ne)]
    ends = [starts[e] + cnts[e] for e in range(ne)]
    a0 = [(starts[e] // sub) * sub for e in range(ne)]
    last_grp = [(ends[e] // sub) * sub for e in range(ne)]
    n_rounds = jnp.int32(0)
    for e in range(ne):
        n_rounds = jnp.maximum(n_rounds, jnp.where(cnts[e] > 0, (ends[e] - a0[e] + rnd - 1) // rnd, 0))
    h = h_ref[...]
    gates = gate_ref[...]
    pos = pos_ref[...]
    row = lax.broadcasted_iota(I32, (rnd, TT), 0)
    grp_row = lax.broadcasted_iota(I32, (sub, xbuf.shape[1]), 0)

    def copy(e, r):
        dst = pl.ds(pl.multiple_of(a0[e] + r * rnd, sub), rnd)
        return pltpu.make_async_copy(xbuf.at[pl.ds(e * rnd, rnd)], xe_ref.at[e, dst], sems.at[e])

    def one_round(r):
        sel = jnp.concatenate(
            [jnp.where(pos[e:e + 1, :] - (a0[e] + r * rnd) == row, 1.0, 0.0).astype(BF16)
             for e in range(ne)], axis=0)
        xbuf[:, 0:d] = _dot(sel, h)
        xbuf[:, d:] = _dot(sel, gates)
        if r == 0:
            for e in range(ne):
                xbuf[e * rnd:e * rnd + sub, :] += carry[e]

        for e in range(ne):
            active = jnp.logical_and(cnts[e] > 0, a0[e] + r * rnd < ends[e])
            partial = jnp.logical_and(ends[e] > last_grp[e], (last_grp[e] - a0[e]) // rnd == r)

            @pl.when(jnp.logical_and(cnts[e] > 0, partial))
            def _():
                off = pl.multiple_of(e * rnd + (last_grp[e] - a0[e]) - r * rnd, sub)
                carry[e] = jnp.where(grp_row < ends[e] - last_grp[e], xbuf[pl.ds(off, sub), :], 0.0)

            @pl.when(active)
            def _():
                copy(e, r).start()

        for e in range(ne):
            @pl.when(jnp.logical_and(cnts[e] > 0, a0[e] + r * rnd < ends[e]))
            def _():
                copy(e, r).wait()

    for r in range((TT + sub - 1 + rnd - 1) // rnd):
        pl.when(r < n_rounds)(functools.partial(one_round, r))
    for e in range(ne):
        @pl.when(jnp.logical_and(cnts[e] > 0, ends[e] == last_grp[e]))
        def _():
            carry[e] = jnp.zeros((sub, xbuf.shape[1]), F32)


def _dispatch(starts, cnts, h2, pos, gate3, slot_rows):
    b, s, d = h2.shape
    nt = s // TT
    lat_tiles = nt - 1
    n_lat_steps = b * lat_tiles
    steps = n_lat_steps + b
    decode = functools.partial(_step_to_tile, n_lat_steps=n_lat_steps, lat_tiles=lat_tiles)

    def tile(step, *_):
        bi, t = decode(step)
        return bi, t, 0

    def pos_tile(step, *_):
        bi, t = decode(step)
        return bi, 0, t

    rows = N_EXPERTS * DISPATCH_ROUND
    width = d + LANES
    return pl.pallas_call(
        functools.partial(_dispatch_kernel, n_lat_steps=n_lat_steps, lat_tiles=lat_tiles, nt=nt, d=d,
                          slot_rows=slot_rows),
        grid_spec=pltpu.PrefetchScalarGridSpec(
            num_scalar_prefetch=2,
            grid=(steps,),
            in_specs=[pl.BlockSpec((None, TT, d), tile),
                      pl.BlockSpec((None, N_EXPERTS, TT), pos_tile),
                      pl.BlockSpec((None, TT, LANES), tile)],
            out_specs=pl.BlockSpec(memory_space=pl.ANY),
            scratch_shapes=[pltpu.VMEM((rows, width), F32),
                            pltpu.VMEM((N_EXPERTS, SUBLANES, width), F32),
                            pltpu.SemaphoreType.DMA((N_EXPERTS,))]),
        out_shape=jax.ShapeDtypeStruct((N_EXPERTS, slot_rows + DISPATCH_ROUND, width), F32),
        compiler_params=_cparams(1),
        name="dispatch",
    )(starts, cnts, h2, pos, gate3)


def _ffn_kernel(x_ref, wg_ref, wu_ref, wd_ref, y_ref, *, d):
    e = pl.program_id(0)
    x = x_ref[:, 0:d].astype(BF16)
    gt = _dot(x, wg_ref[...])
    up = _dot(x, wu_ref[...])
    hid = (gt * jax.nn.sigmoid(gt) * up).astype(BF16)
    y = _dot(hid, wd_ref[...])
    g3 = x_ref[:, d:]
    lane = lax.broadcasted_iota(I32, g3.shape, 1)
    mine = jnp.logical_and(lane % N_EXPERTS == e, lane < 3 * N_EXPERTS)
    gate = jnp.sum(jnp.where(mine, g3, 0.0), axis=1, keepdims=True)
    y_ref[...] = (y * gate).astype(y_ref.dtype)


def _ffn_tile(rows):
    best = BF16_ROWS
    for m in range(BF16_ROWS, 641, BF16_ROWS):
        if rows % m == 0:
            best = m
    return best


def _ffn(xe, wg, wu, wd, slot_rows):
    ne, _, width = xe.shape
    d, hdim = wg.shape[-2:]
    mt = _ffn_tile(slot_rows)
    wspec = lambda shape: pl.BlockSpec((None,) + shape, lambda e, j: (e, 0, 0))
    return pl.pallas_call(
        functools.partial(_ffn_kernel, d=d),
        grid=(ne, slot_rows // mt),
        in_specs=[pl.BlockSpec((None, mt, width), lambda e, j: (e, j, 0)),
                  wspec((d, hdim)), wspec((d, hdim)), wspec((hdim, d))],
        out_specs=pl.BlockSpec((None, mt, d), lambda e, j: (e, j, 0)),
        out_shape=jax.ShapeDtypeStruct((ne, slot_rows, d), BF16),
        compiler_params=_cparams(2),
        name="ffn",
    )(xe, wg, wu, wd)


def _combine_kernel(start_sm, cnt_sm, x_ref, post_ref, mod_ref, fg_ref, y_ref, o_ref, ybuf, acc_ref, sems,
                    *, d, nt, slot_rows, final):
    bi = pl.program_id(0)
    t = pl.program_id(1) + (1 if final else 0)
    ne = N_EXPERTS
    win = COMBINE_WINDOW
    sbase = (bi * nt + t) * ne
    starts = [start_sm[sbase + e] for e in range(ne)]
    cmax = cnt_sm[sbase]
    for e in range(1, ne):
        cmax = jnp.maximum(cmax, cnt_sm[sbase + e])
    post = post_ref[...]
    lane = lax.broadcasted_iota(I32, (TT, win), 1)

    def one_round(r):
        a = [jnp.minimum((starts[e] // BF16_ROWS) * BF16_ROWS + r * COMBINE_ROUND, slot_rows - win)
             for e in range(ne)]
        cps = [pltpu.make_async_copy(y_ref.at[e, pl.ds(pl.multiple_of(a[e], BF16_ROWS), win)],
                                     ybuf.at[pl.ds(e * win, win)], sems.at[e]) for e in range(ne)]
        for cp in cps:
            cp.start()
        sel = []
        for e in range(ne):
            rel = post[:, e:e + 1] - starts[e]
            in_round = jnp.logical_and(rel >= r * COMBINE_ROUND, rel < (r + 1) * COMBINE_ROUND)
            hit = jnp.logical_and(in_round, post[:, e:e + 1] - a[e] == lane)
            sel.append(jnp.where(hit, 1.0, 0.0).astype(BF16))
        sel = jnp.concatenate(sel, axis=1)
        for cp in cps:
            cp.wait()
        acc_ref[...] += _dot(sel, ybuf[...])

    n_rounds = (cmax + COMBINE_ROUND - 1) // COMBINE_ROUND
    acc_ref[...] = jnp.zeros_like(acc_ref)
    for r in range((TT + COMBINE_ROUND - 1) // COMBINE_ROUND):
        pl.when(r < n_rounds)(functools.partial(one_round, r))
    x2 = x_ref[...] + mod_ref[:, 5 * d:6 * d] * acc_ref[...]
    if final:
        x2 = x2 * lax.rsqrt(jnp.mean(x2 * x2, axis=-1, keepdims=True) + EPS) * fg_ref[...]
    o_ref[...] = x2


def _combine(starts, cnts, x1, pos_t, modsel, final_g, y, slot_rows, final):
    b, s, d = x1.shape
    nt = s // TT
    t0 = 1 if final else 0
    tile = lambda bi, t, *_: (bi, t + t0, 0)
    out_rows = s - t0 * TT if final else s
    out_tile = (lambda bi, t, *_: (bi, t, 0)) if final else tile
    return pl.pallas_call(
        functools.partial(_combine_kernel, d=d, nt=nt, slot_rows=slot_rows, final=final),
        grid_spec=pltpu.PrefetchScalarGridSpec(
            num_scalar_prefetch=2,
            grid=(b, nt - t0),
            in_specs=[pl.BlockSpec((None, TT, d), tile),
                      pl.BlockSpec((None, TT, LANES), tile),
                      pl.BlockSpec((None, None, 1, 6 * d), lambda bi, t, *_: (bi, jnp.minimum(t + t0, 1), 0, 0)),
                      pl.BlockSpec((1, d), lambda bi, t, *_: (0, 0)),
                      pl.BlockSpec(memory_space=pl.ANY)],
            out_specs=pl.BlockSpec((None, TT, d), out_tile),
            scratch_shapes=[pltpu.VMEM((N_EXPERTS * COMBINE_WINDOW, d), BF16),
                            pltpu.VMEM((TT, d), F32),
                            pltpu.SemaphoreType.DMA((N_EXPERTS,))]),
        out_shape=jax.ShapeDtypeStruct((b, out_rows, d), F32),
        compiler_params=_cparams(2),
        name="combine",
    )(starts, cnts, x1, pos_t, modsel, final_g, y)


def _rope_tables(n, ctx):
    tkn = np.arange(n)
    half = HEAD_DIM // 4
    inv = ROPE_THETA ** (-np.arange(half, dtype=np.float64) / half)
    ang_row = (tkn // GRID_W)[:, None] * inv[None, :]
    ang_col = (tkn % GRID_W)[:, None] * inv[None, :]
    ang = np.concatenate([ang_row, ang_row, ang_col, ang_col], axis=1)
    sign = np.tile(np.concatenate([-np.ones(half), np.ones(half)]), 2)
    cos = np.concatenate([np.ones((ctx, HEAD_DIM)), np.cos(ang)], axis=0)
    sin = np.concatenate([np.zeros((ctx, HEAD_DIM)), np.sin(ang) * sign[None, :]], axis=0)
    reps = LANES // HEAD_DIM
    return (jnp.asarray(np.tile(cos, (1, reps)), F32), jnp.asarray(np.tile(sin, (1, reps)), F32))


def _block_diag_ones(width):
    idx = np.arange(width) // HEAD_DIM
    return jnp.asarray(idx[:, None] == idx[None, :], BF16)


def kernel(x, c, ctx, c_ctx, w_ada, b_ada, norm1_g, norm2_g, w_in, q_norm_g, k_norm_g, na_rpb, conv_w, conv_b,
           conv_ln_g, conv_ln_b, w_out, w_router, w_gate, w_up, w_down, final_norm_g):
    b, n, d = x.shape
    nctx = ctx.shape[1]
    depth = w_ada.shape[0]
    assert nctx == TT and n % TT == 0 and n % GRID_W == 0 and b <= SUBLANES - 1
    assert w_router.shape[-1] == N_EXPERTS and w_in.shape[-1] == IN_WIDTH
    s = nctx + n
    nt = s // TT
    cap_lat = CAPACITY_FACTOR * n // N_EXPERTS
    cap_ctx = CAPACITY_FACTOR * nctx // N_EXPERTS
    assert cap_ctx <= DISPATCH_ROUND and cap_lat % BF16_ROWS == 0
    lat_rows = b * cap_lat
    ctx_rows = b * DISPATCH_ROUND

    cond = jnp.zeros((SUBLANES, d), F32).at[:b].set(c).at[b].set(c_ctx)
    mods = _ada(cond, w_ada, b_ada)
    cos, sin = _rope_tables(n, nctx)
    ones_q = _block_diag_ones(A_WIDTH)
    ones_k = _block_diag_ones(A_KV_WIDTH)
    tri = jnp.asarray(np.triu(np.ones((TT, TT))), BF16)

    stream = jnp.concatenate([ctx, x], axis=1)
    slot_rows = lat_rows + ctx_rows
    segments = ((0, 1, cap_ctx, lat_rows, DISPATCH_ROUND), (1, nt - 1, cap_lat, 0, cap_lat))
    bias_rows = n // GRID_W
    for i in range(depth):
        m = mods[i]
        modsel = jnp.stack([jnp.broadcast_to(m[b], (b, 6 * d)), m[:b]], axis=1)[:, :, None, :]
        qg = jnp.tile(q_norm_g[i], A_HEADS)[None, :]
        kg = jnp.tile(k_norm_g[i], A_KV_HEADS)[None, :]
        qa, ka, va, qb, kb, vb, u = _in_proj(stream, modsel, norm1_g[i][None, :], w_in[i].astype(BF16),
                                             cos, sin, qg, kg, ones_q, ones_k)
        a_mix = _attn_a(qa, ka, va, nctx)
        b_mix = _attn_b(qb, kb, vb, _na_bias_tables(na_rpb[i], bias_rows), nctx)
        wr = jnp.zeros((d, LANES), F32).at[:, :N_EXPERTS].set(w_router[i])
        wr_hi = wr.astype(BF16)
        wr_lo = (wr - wr_hi.astype(F32)).astype(BF16)
        x1, h2, aff_t, gate3 = _mix_out(a_mix, b_mix, u, stream, modsel, conv_w[i], conv_b[i][None, :],
                                        conv_ln_g[i][None, :], conv_ln_b[i][None, :], w_out[i].astype(BF16),
                                        norm2_g[i][None, :], wr_hi, wr_lo)
        pos, pos_t, start_t, cnt_t = _route(aff_t, tri, segments)
        starts = jnp.transpose(start_t, (0, 2, 1))[:, :nt, :].reshape(-1)
        cnts = jnp.transpose(cnt_t, (0, 2, 1))[:, :nt, :].reshape(-1)
        xe = _dispatch(starts, cnts, h2, pos, gate3, slot_rows)
        y = _ffn(xe, w_gate[i].astype(BF16), w_up[i].astype(BF16), w_down[i].astype(BF16), slot_rows)
        stream = _combine(starts, cnts, x1, pos_t, modsel, final_norm_g[None, :], y, slot_rows, i == depth - 1)
    return stream
```

```python
import functools

import numpy as np
import jax
import jax.numpy as jnp
from jax import lax
from jax.experimental import pallas as pl
from jax.experimental.pallas import tpu as pltpu

F32 = jnp.float32
BF16 = jnp.bfloat16
I32 = jnp.int32

GRID_W = 64
HEAD_DIM = 64
A_HEADS = 8
A_KV_HEADS = 2
A_GROUP = A_HEADS // A_KV_HEADS
B_HEADS = 4
C_CHANNELS = 256
CONV_WIDTH = 31
NA_WIN_ROWS = 8
NA_WIN_COLS = 16
N_EXPERTS = 16
CAPACITY_FACTOR = 2
ROPE_THETA = 10000.0
EPS = 1e-6
A_WIDTH = A_HEADS * HEAD_DIM
A_KV_WIDTH = A_KV_HEADS * HEAD_DIM
B_WIDTH = B_HEADS * HEAD_DIM
MIX_WIDTH = A_WIDTH + B_WIDTH + C_CHANNELS
IN_WIDTH = A_WIDTH + 2 * A_KV_WIDTH + 3 * B_WIDTH + 2 * C_CHANNELS

LANES = 128
SUBLANES = 8
BF16_ROWS = 16
VMEM_LIMIT = 56 * 1024 * 1024

TT = 256
NEG = -1e30
LOG2_E = float(np.log2(np.e))
CONV_HALO = 16
DISPATCH_ROUND = 64
COMBINE_WINDOW = 128
COMBINE_ROUND = COMBINE_WINDOW - BF16_ROWS


def _cparams(n_axes):
    return pltpu.CompilerParams(dimension_semantics=("arbitrary",) * n_axes, vmem_limit_bytes=VMEM_LIMIT)


def _split3(x):
    hi = x.astype(BF16)
    r1 = x - hi.astype(F32)
    mid = r1.astype(BF16)
    lo = (r1 - mid.astype(F32)).astype(BF16)
    return hi, mid, lo


def _dot(a, b):
    return jnp.dot(a, b, preferred_element_type=F32)


def _dot_nt(a, b):
    return lax.dot_general(a, b, (((1,), (1,)), ((), ())), preferred_element_type=F32)


def _dot3(a, b_hi, b_lo):
    a_hi = a.astype(BF16)
    a_lo = (a - a_hi.astype(F32)).astype(BF16)
    return _dot(a_hi, b_hi) + (_dot(a_lo, b_hi) + _dot(a_hi, b_lo))


def _ada_kernel(cond_ref, w_ref, b_ref, o_ref):
    c = cond_ref[...]
    s = c * jax.nn.sigmoid(c)
    w = w_ref[0]
    w_hi = w.astype(BF16)
    w_lo = (w - w_hi.astype(F32)).astype(BF16)
    o_ref[0] = _dot3(s, w_hi, w_lo) + b_ref[0]


def _ada(cond, w_ada, b_ada):
    depth, d, six_d = w_ada.shape
    bn = six_d // 4
    return pl.pallas_call(
        _ada_kernel,
        grid=(depth, six_d // bn),
        in_specs=[pl.BlockSpec((SUBLANES, d), lambda l, j: (0, 0)),
                  pl.BlockSpec((1, d, bn), lambda l, j: (l, 0, j)),
                  pl.BlockSpec((1, 1, bn), lambda l, j: (l, 0, j))],
        out_specs=pl.BlockSpec((1, SUBLANES, bn), lambda l, j: (l, 0, j)),
        out_shape=jax.ShapeDtypeStruct((depth, SUBLANES, six_d), F32),
        compiler_params=_cparams(2),
        name="ada",
    )(cond, w_ada, b_ada.reshape(depth, 1, six_d))


def _head_rms(x, ones_bd, g):
    sq = x * x
    hi = sq.astype(BF16)
    lo = (sq - hi.astype(F32)).astype(BF16)
    ssum = _dot(hi, ones_bd) + _dot(lo, ones_bd)
    return x * lax.rsqrt(ssum * (1.0 / HEAD_DIM) + EPS) * g


def _rope(x, cos, sin_signed):
    w = x.shape[-1]
    lane = lax.broadcasted_iota(I32, x.shape, 1)
    first_half = (lane % 32) < 16
    partner = jnp.where(first_half, pltpu.roll(x, w - 16, 1), pltpu.roll(x, 16, 1))
    return x * cos + partner * sin_signed


def _in_proj_kernel(x_ref, mod_ref, g1_ref, w_ref, cos_ref, sin_ref, qg_ref, kg_ref, onesq_ref, onesk_ref,
                    qa_ref, ka_ref, va_ref, qb_ref, kb_ref, vb_ref, u_ref, *, d):
    x = x_ref[...]
    y = x * lax.rsqrt(jnp.mean(x * x, axis=-1, keepdims=True) + EPS) * g1_ref[...]
    shift = mod_ref[:, 0:d]
    scale = mod_ref[:, d:2 * d]
    h = (y * (1.0 + scale) + shift).astype(BF16)
    p = _dot(h, w_ref[...])
    o = 0
    qa = p[:, o:o + A_WIDTH]; o += A_WIDTH
    ka = p[:, o:o + A_KV_WIDTH]; o += A_KV_WIDTH
    va = p[:, o:o + A_KV_WIDTH]; o += A_KV_WIDTH
    qb = p[:, o:o + B_WIDTH]; o += B_WIDTH
    kb = p[:, o:o + B_WIDTH]; o += B_WIDTH
    vb = p[:, o:o + B_WIDTH]; o += B_WIDTH
    cv = p[:, o:o + C_CHANNELS]; o += C_CHANNELS
    cg = p[:, o:o + C_CHANNELS]

    cos = cos_ref[...]
    sin = sin_ref[...]
    sm_scale = HEAD_DIM ** -0.5
    qa = _head_rms(qa, onesq_ref[...], qg_ref[...])
    qa = _rope(qa, jnp.concatenate([cos] * (A_WIDTH // LANES), axis=1),
               jnp.concatenate([sin] * (A_WIDTH // LANES), axis=1)) * (sm_scale * LOG2_E)
    ka = _head_rms(ka, onesk_ref[...], kg_ref[...])
    ka = _rope(ka, cos, sin)
    qa_ref[...] = qa.T.reshape(A_HEADS, HEAD_DIM, -1).astype(BF16)
    va_ref[...] = va.T.reshape(A_KV_HEADS, HEAD_DIM, -1).astype(BF16)
    for hh in range(A_KV_HEADS):
        ka_ref[hh] = ka[:, hh * HEAD_DIM:(hh + 1) * HEAD_DIM].astype(BF16)
    qb_ref[...] = (qb * sm_scale).astype(BF16)
    kb_ref[...] = kb.astype(BF16)
    vb_ref[...] = vb.astype(BF16)
    u_ref[...] = cv * jax.nn.sigmoid(cg)


def _in_proj(x, modsel, g1, w_in, cos, sin, qg, kg, ones_q, ones_k):
    b, s, d = x.shape
    nt = s // TT
    tile = lambda bi, t: (bi, t, 0)
    head_tile = lambda bi, t: (bi, 0, t, 0)
    head_tile_t = lambda bi, t: (bi, 0, 0, t)
    const2 = lambda bi, t: (0, 0)
    return pl.pallas_call(
        functools.partial(_in_proj_kernel, d=d),
        grid=(b, nt),
        in_specs=[pl.BlockSpec((None, TT, d), tile),
                  pl.BlockSpec((None, None, 1, 6 * d), lambda bi, t: (bi, jnp.minimum(t, 1), 0, 0)),
                  pl.BlockSpec((1, d), const2),
                  pl.BlockSpec((d, IN_WIDTH), const2),
                  pl.BlockSpec((TT, LANES), lambda bi, t: (t, 0)),
                  pl.BlockSpec((TT, LANES), lambda bi, t: (t, 0)),
                  pl.BlockSpec((1, A_WIDTH), const2),
                  pl.BlockSpec((1, A_KV_WIDTH), const2),
                  pl.BlockSpec((A_WIDTH, A_WIDTH), const2),
                  pl.BlockSpec((A_KV_WIDTH, A_KV_WIDTH), const2)],
        out_specs=[pl.BlockSpec((None, A_HEADS, HEAD_DIM, TT), head_tile_t),
                   pl.BlockSpec((None, A_KV_HEADS, TT, HEAD_DIM), head_tile),
                   pl.BlockSpec((None, A_KV_HEADS, HEAD_DIM, TT), head_tile_t),
                   pl.BlockSpec((None, TT, B_WIDTH), tile),
                   pl.BlockSpec((None, TT, B_WIDTH), tile),
                   pl.BlockSpec((None, TT, B_WIDTH), tile),
                   pl.BlockSpec((None, TT, C_CHANNELS), tile)],
        out_shape=[jax.ShapeDtypeStruct((b, A_HEADS, HEAD_DIM, s), BF16),
                   jax.ShapeDtypeStruct((b, A_KV_HEADS, s, HEAD_DIM), BF16),
                   jax.ShapeDtypeStruct((b, A_KV_HEADS, HEAD_DIM, s), BF16),
                   jax.ShapeDtypeStruct((b, s, B_WIDTH), BF16),
                   jax.ShapeDtypeStruct((b, s, B_WIDTH), BF16),
                   jax.ShapeDtypeStruct((b, s, B_WIDTH), BF16),
                   jax.ShapeDtypeStruct((b, s, C_CHANNELS), F32)],
        compiler_params=_cparams(2),
        name="in_proj",
    )(x, modsel, g1, w_in, cos, sin, qg, kg, ones_q, ones_k)


def _attn_a_kernel(q_ref, k_ref, v_ref, o_ref, *, ctx, tk, n_chunks):
    t = pl.program_id(2)
    n_heads, dh, tq = q_ref.shape
    g = n_heads // 2
    qs = [jnp.concatenate([q_ref[2 * i], q_ref[2 * i + 1]], axis=1) for i in range(g)]

    def update(state, s, v):
        m, l, acc = state
        m_new = jnp.maximum(m, jnp.max(s, axis=0, keepdims=True))
        alpha = jnp.exp2(m - m_new)
        p = jnp.exp2(s - m_new)
        l = alpha * l + jnp.sum(p, axis=0, keepdims=True)
        return m_new, l, alpha * acc + _dot(v, p.astype(BF16))

    def step(s0, k, v, k_next, states):
        out = []
        s = s0
        for i in range(g):
            s_next = _dot(k, qs[i + 1]) if i + 1 < g else _dot(k_next, qs[0])
            out.append(update(states[i], s, v))
            s = s_next
        return s, tuple(out)

    def chunk(c):
        return pl.ds(pl.multiple_of(ctx + c * tk, int(np.gcd(ctx, tk))), tk)

    init = tuple((jnp.full((1, 2 * tq), NEG, F32), jnp.zeros((1, 2 * tq), F32), jnp.zeros((dh, 2 * tq), F32))
                 for _ in range(g))
    k_ctx = k_ref[0:ctx, :]
    v_ctx = v_ref[:, 0:ctx]
    s0 = _dot(k_ctx, qs[0])

    def finish(states):
        o_t = jnp.concatenate([part for (_, l, acc) in states
                               for part in ((acc / l)[:, 0:tq], (acc / l)[:, tq:2 * tq])], axis=0)
        o_ref[...] = o_t.T.astype(o_ref.dtype)

    @pl.when(t == 0)
    def _():
        _, states = step(s0, k_ctx, v_ctx, k_ctx, init)
        finish(states)

    @pl.when(t > 0)
    def _():
        carry = step(s0, k_ctx, v_ctx, k_ref[chunk(0), :], init)

        def body(c, carry):
            s_pair0, states = carry
            nxt = jnp.minimum(c + 1, n_chunks - 1)
            return step(s_pair0, k_ref[chunk(c), :], v_ref[:, chunk(c)], k_ref[chunk(nxt), :], states)

        finish(lax.fori_loop(0, n_chunks, body, carry, unroll=True)[1])


def _attn_a(qa_t, ka, va_t, ctx):
    b, _, dh, s = qa_t.shape
    nt = s // TT
    tk = 1024 if (s - ctx) % 1024 == 0 else TT
    return pl.pallas_call(
        functools.partial(_attn_a_kernel, ctx=ctx, tk=tk, n_chunks=(s - ctx) // tk),
        grid=(b, A_KV_HEADS, nt),
        in_specs=[pl.BlockSpec((None, A_GROUP, dh, TT), lambda bi, j, t: (bi, j, 0, t)),
                  pl.BlockSpec((None, None, s, dh), lambda bi, j, t: (bi, j, 0, 0)),
                  pl.BlockSpec((None, None, dh, s), lambda bi, j, t: (bi, j, 0, 0))],
        out_specs=pl.BlockSpec((None, TT, A_GROUP * dh), lambda bi, j, t: (bi, t, j)),
        out_shape=jax.ShapeDtypeStruct((b, s, A_WIDTH), BF16),
        compiler_params=_cparams(3),
        name="attn_a",
    )(qa_t, ka, va_t)


def _attn_b_kernel(q_ref, k_ref, v_ref, bias_ref, o_ref, *, ctx, rows, wr):
    t = pl.program_id(1)
    tt, w = q_ref.shape
    rows_per_tile = tt // GRID_W
    win = wr * GRID_W
    q = q_ref[...]
    kc = k_ref[0:ctx, :]
    vc = v_ref[0:ctx, :]
    lane_head = lax.broadcasted_iota(I32, (tt, w), 1) // HEAD_DIM

    def softmax_parts(parts):
        m = parts[0].max(axis=1, keepdims=True)
        for s in parts[1:]:
            m = jnp.maximum(m, s.max(axis=1, keepdims=True))
        es = [jnp.exp(s - m) for s in parts]
        tot = es[0].sum(axis=1, keepdims=True)
        for e in es[1:]:
            tot = tot + e.sum(axis=1, keepdims=True)
        inv = 1.0 / tot
        return [(e * inv).astype(BF16) for e in es]

    @pl.when(t == 0)
    def _():
        out = jnp.zeros((tt, w), F32)
        for hh in range(B_HEADS):
            qh = jnp.where(lane_head == hh, q, jnp.zeros_like(q))
            (p,) = softmax_parts([_dot_nt(qh, kc)])
            out = out + jnp.where(lane_head == hh, _dot(p, vc), 0.0)
        o_ref[...] = out.astype(o_ref.dtype)

    @pl.when(t > 0)
    def _():
        out = jnp.zeros((tt, w), F32)
        r0 = (t - 1) * rows_per_tile
        for hh in range(B_HEADS):
            qh = jnp.where(lane_head == hh, q, jnp.zeros_like(q))
            s_cx = _dot_nt(qh, kc)
            s_nb = []
            v_rows = []
            for i in range(rows_per_tile):
                r = r0 + i
                rs = jnp.clip(r - wr // 2, 0, rows - wr)
                start = pl.multiple_of(ctx + rs * GRID_W, GRID_W)
                k_rows = k_ref[pl.ds(start, win), :]
                v_rows.append(v_ref[pl.ds(start, win), :])
                s_nb.append(_dot_nt(qh[i * GRID_W:(i + 1) * GRID_W], k_rows) + bias_ref[r - rs, hh])
            p_cx, p_nb = softmax_parts([s_cx, jnp.concatenate(s_nb, axis=0)])
            o_cx = _dot(p_cx, vc)
            o_nb = jnp.concatenate([_dot(p_nb[i * GRID_W:(i + 1) * GRID_W], v_rows[i])
                                    for i in range(rows_per_tile)], axis=0)
            out = out + jnp.where(lane_head == hh, o_cx + o_nb, 0.0)
        o_ref[...] = out.astype(o_ref.dtype)


def _attn_b(qb, kb, vb, bias, ctx):
    b, s, w = qb.shape
    nt = s // TT
    rows = (s - ctx) // GRID_W
    wr = min(NA_WIN_ROWS, rows)
    tile = lambda bi, t: (bi, t, 0)
    whole = lambda bi, t: (bi, 0, 0)
    return pl.pallas_call(
        functools.partial(_attn_b_kernel, ctx=ctx, rows=rows, wr=wr),
        grid=(b, nt),
        in_specs=[pl.BlockSpec((None, TT, w), tile),
                  pl.BlockSpec((None, s, w), whole),
                  pl.BlockSpec((None, s, w), whole),
                  pl.BlockSpec(bias.shape, lambda bi, t: (0, 0, 0, 0))],
        out_specs=pl.BlockSpec((None, TT, w), tile),
        out_shape=jax.ShapeDtypeStruct((b, s, w), BF16),
        compiler_params=_cparams(2),
        name="attn_b",
    )(qb, kb, vb, bias)


def _na_bias_tables(rpb, rows):
    wr = min(NA_WIN_ROWS, rows)
    wc = NA_WIN_COLS
    q = np.arange(GRID_W)
    cs = np.clip(q - wc // 2, 0, GRID_W - wc)
    kc = np.arange(GRID_W)
    valid = (kc[None, :] >= cs[:, None]) & (kc[None, :] < cs[:, None] + wc)
    dc = np.clip(kc[None, :] - q[:, None] + (NA_WIN_COLS - 1), 0, 2 * NA_WIN_COLS - 2)
    v = np.arange(wr)
    wrow = np.arange(wr)
    dr = wrow[None, :] - v[:, None] + (NA_WIN_ROWS - 1)
    pick_r = jnp.asarray(dr[None, :, :] == np.arange(2 * NA_WIN_ROWS - 1)[:, None, None], F32)
    pick_c = jnp.asarray((dc[None, :, :] == np.arange(2 * NA_WIN_COLS - 1)[:, None, None])
                         & valid[None, :, :], F32)
    tab = jnp.einsum("hrc,rvw,cqk->vhqwk", rpb.astype(F32), pick_r, pick_c, precision=lax.Precision.HIGHEST)
    tab = tab + jnp.asarray(np.where(valid, 0.0, NEG), F32)[None, None, :, None, :]
    return tab.reshape(wr, B_HEADS, GRID_W, wr * GRID_W)


def _mix_out_kernel(a_ref, b_ref, up_ref, uc_ref, un_ref, x_ref, mod_ref, cw_ref, cb_ref, lg_ref, lb_ref,
                    wo_ref, g2_ref, wrh_ref, wrl_ref,
                    x1_ref, h2_ref, aff_ref, gate_ref, *, d, nt):
    t = pl.program_id(1)
    tt = uc_ref.shape[0]
    prev_ok = t >= 2
    next_ok = jnp.logical_and(t >= 1, t < nt - 1)
    prev = jnp.where(prev_ok, up_ref[tt - CONV_HALO:tt, :], 0.0)
    nxt = jnp.where(next_ok, un_ref[0:CONV_HALO, :], 0.0)
    win = jnp.concatenate([prev, uc_ref[...], nxt], axis=0)
    off = CONV_HALO - CONV_WIDTH // 2
    acc = jnp.zeros((tt, C_CHANNELS), F32) + cb_ref[...]
    for k in range(CONV_WIDTH):
        acc = acc + win[off + k:off + k + tt, :] * cw_ref[k:k + 1, :]
    mu = jnp.mean(acc, axis=-1, keepdims=True)
    var = jnp.mean(jnp.square(acc - mu), axis=-1, keepdims=True)
    yc = (acc - mu) * lax.rsqrt(var + EPS) * lg_ref[...] + lb_ref[...]
    c = (yc * jax.nn.sigmoid(yc)).astype(BF16)

    mix = (_dot(a_ref[...], wo_ref[0:A_WIDTH, :])
           + _dot(b_ref[...], wo_ref[A_WIDTH:A_WIDTH + B_WIDTH, :])
           + _dot(c, wo_ref[A_WIDTH + B_WIDTH:MIX_WIDTH, :]))
    g1 = mod_ref[:, 2 * d:3 * d]
    x1 = x_ref[...] + g1 * mix
    x1_ref[...] = x1
    y = x1 * lax.rsqrt(jnp.mean(x1 * x1, axis=-1, keepdims=True) + EPS) * g2_ref[...]
    h2 = y * (1.0 + mod_ref[:, 4 * d:5 * d]) + mod_ref[:, 3 * d:4 * d]
    h2_ref[...] = h2.astype(BF16)
    logits = _dot3(h2, wrh_ref[...], wrl_ref[...])
    lane = lax.broadcasted_iota(I32, logits.shape, 1)
    logits = jnp.where(lane < N_EXPERTS, logits, NEG)
    e = jnp.exp(logits - jnp.max(logits, axis=-1, keepdims=True))
    aff = e / jnp.sum(e, axis=-1, keepdims=True)
    aff_ref[...] = aff.T[0:N_EXPERTS, :]
    hi, mid, lo = _split3(aff)
    gate = jnp.where(lane < N_EXPERTS, hi.astype(F32),
                     jnp.where(lane < 2 * N_EXPERTS, pltpu.roll(mid.astype(F32), N_EXPERTS, 1),
                               jnp.where(lane < 3 * N_EXPERTS, pltpu.roll(lo.astype(F32), 2 * N_EXPERTS, 1), 0.0)))
    gate_ref[...] = gate.astype(BF16)


def _mix_out(a, bmix, u, x, modsel, cw, cb, lg, lb, wo, g2, wr_hi, wr_lo):
    b, s, d = x.shape
    nt = s // TT
    tile = lambda bi, t: (bi, t, 0)
    const2 = lambda bi, t: (0, 0)
    return pl.pallas_call(
        functools.partial(_mix_out_kernel, d=d, nt=nt),
        grid=(b, nt),
        in_specs=[pl.BlockSpec((None, TT, A_WIDTH), tile),
                  pl.BlockSpec((None, TT, B_WIDTH), tile),
                  pl.BlockSpec((None, TT, C_CHANNELS), lambda bi, t: (bi, jnp.maximum(t - 1, 0), 0)),
                  pl.BlockSpec((None, TT, C_CHANNELS), tile),
                  pl.BlockSpec((None, TT, C_CHANNELS), lambda bi, t: (bi, jnp.minimum(t + 1, nt - 1), 0)),
                  pl.BlockSpec((None, TT, d), tile),
                  pl.BlockSpec((None, None, 1, 6 * d), lambda bi, t: (bi, jnp.minimum(t, 1), 0, 0)),
                  pl.BlockSpec((CONV_WIDTH, C_CHANNELS), const2),
                  pl.BlockSpec((1, C_CHANNELS), const2),
                  pl.BlockSpec((1, C_CHANNELS), const2),
                  pl.BlockSpec((1, C_CHANNELS), const2),
                  pl.BlockSpec((MIX_WIDTH, d), const2),
                  pl.BlockSpec((1, d), const2),
                  pl.BlockSpec((d, LANES), const2),
                  pl.BlockSpec((d, LANES), const2)],
        out_specs=[pl.BlockSpec((None, TT, d), tile),
                   pl.BlockSpec((None, TT, d), tile),
                   pl.BlockSpec((None, N_EXPERTS, TT), lambda bi, t: (bi, 0, t)),
                   pl.BlockSpec((None, TT, LANES), tile)],
        out_shape=[jax.ShapeDtypeStruct((b, s, d), F32),
                   jax.ShapeDtypeStruct((b, s, d), BF16),
                   jax.ShapeDtypeStruct((b, N_EXPERTS, s), F32),
                   jax.ShapeDtypeStruct((b, s, LANES), BF16)],
        compiler_params=_cparams(2),
        name="mix_out",
    )(a, bmix, u, u, u, x, modsel, cw, cb, lg, lb, wo, g2, wr_hi, wr_lo)


def _route_kernel(aff_ref, tri_ref, pos_ref, post_ref, start_ref, cnt_ref, *, segments):
    bi = pl.program_id(0)
    tri = tri_ref[...]
    ne = N_EXPERTS
    lane_t = lax.broadcasted_iota(I32, (ne, LANES), 1)
    starts = jnp.zeros((ne, LANES), F32)
    cnts = jnp.zeros((ne, LANES), F32)

    def block_cumsum(mask_f32, carry):
        inc = _dot(mask_f32.astype(BF16), tri) + carry
        return inc - mask_f32, inc[:, TT - 1:TT]

    for (tile0, ntiles, cap, row0, rows_per_sample) in segments:
        lo = tile0 * TT
        n = ntiles * TT
        bits = pltpu.bitcast(aff_ref[:, lo:lo + n], I32)

        def search(i, cur):
            cand = cur | lax.shift_left(jnp.int32(1), 30 - i)
            c = jnp.sum(jnp.where(bits >= cand, 1.0, 0.0), axis=1, keepdims=True)
            return jnp.where(c >= cap, cand, cur)

        thr = lax.fori_loop(0, 31, search, jnp.zeros((ne, 1), I32))
        n_gt = jnp.sum(jnp.where(bits > thr, 1.0, 0.0), axis=1, keepdims=True)
        need = cap - n_gt
        base = (row0 + bi * rows_per_sample).astype(F32)
        tie_carry = jnp.zeros((ne, 1), F32)
        pos_carry = jnp.zeros((ne, 1), F32)
        for k in range(ntiles):
            bk = bits[:, k * TT:(k + 1) * TT]
            eq = jnp.where(bk == thr, 1.0, 0.0)
            tie_rank, tie_carry = block_cumsum(eq, tie_carry)
            sel = jnp.where(bk > thr, 1.0, jnp.where(tie_rank < need, eq, 0.0))
            rank, new_carry = block_cumsum(sel, pos_carry)
            pos = jnp.where(sel > 0.0, rank + base, -1.0)
            pos_ref[:, lo + k * TT:lo + (k + 1) * TT] = pos.astype(I32)
            padded = jnp.concatenate([pos, jnp.full((LANES - ne, TT), -1.0, F32)], axis=0)
            post_ref[lo + k * TT:lo + (k + 1) * TT, :] = padded.T.astype(I32)
            starts = jnp.where(lane_t == tile0 + k, pos_carry + base, starts)
            cnts = jnp.where(lane_t == tile0 + k, new_carry - pos_carry, cnts)
            pos_carry = new_carry
    start_ref[...] = starts.astype(I32)
    cnt_ref[...] = cnts.astype(I32)


def _route(aff_t, tri, segments):
    b, ne, s = aff_t.shape
    return pl.pallas_call(
        functools.partial(_route_kernel, segments=segments),
        grid=(b,),
        in_specs=[pl.BlockSpec((None, ne, s), lambda bi: (bi, 0, 0)),
                  pl.BlockSpec((TT, TT), lambda bi: (0, 0))],
        out_specs=[pl.BlockSpec((None, ne, s), lambda bi: (bi, 0, 0)),
                   pl.BlockSpec((None, s, LANES), lambda bi: (bi, 0, 0)),
                   pl.BlockSpec((None, ne, LANES), lambda bi: (bi, 0, 0)),
                   pl.BlockSpec((None, ne, LANES), lambda bi: (bi, 0, 0))],
        out_shape=[jax.ShapeDtypeStruct((b, ne, s), I32),
                   jax.ShapeDtypeStruct((b, s, LANES), I32),
                   jax.ShapeDtypeStruct((b, ne, LANES), I32),
                   jax.ShapeDtypeStruct((b, ne, LANES), I32)],
        compiler_params=_cparams(1),
        name="route",
    )(aff_t, tri)


def _step_to_tile(step, n_lat_steps, lat_tiles):
    is_lat = step < n_lat_steps
    bi = jnp.where(is_lat, step // lat_tiles, step - n_lat_steps)
    t = jnp.where(is_lat, 1 + step % lat_tiles, 0)
    return bi, t


def _dispatch_kernel(start_sm, cnt_sm, h_ref, pos_ref, gate_ref, xe_ref, xbuf, carry, sems,
                     *, n_lat_steps, lat_tiles, nt, d, slot_rows):
    step = pl.program_id(0)
    bi, t = _step_to_tile(step, n_lat_steps, lat_tiles)
    ne = N_EXPERTS
    rnd = DISPATCH_ROUND
    sub = SUBLANES

    @pl.when(step == 0)
    def _():
        carry[...] = jnp.zeros_like(carry)
        xbuf[0:rnd, :] = jnp.zeros((rnd, xbuf.shape[1]), F32)
        pads = [pltpu.make_async_copy(xbuf.at[pl.ds(0, rnd)], xe_ref.at[e, pl.ds(slot_rows, rnd)], sems.at[e])
                for e in range(ne)]
        for cp in pads:
            cp.start()
        for cp in pads:
            cp.wait()

    sbase = (bi * nt + t) * ne
    starts = [start_sm[sbase + e] for e in range(ne)]
    cnts = [cnt_sm[sbase + e] for e in range(ne)]
    ends = [starts[e] + cnts[e] for e in range(ne)]
    a0 = [(starts[e] // sub) * sub for e in range(ne)]
    last_grp = [(ends[e] // sub) * sub for e in range(ne)]
    n_rounds = jnp.int32(0)
    for e in range(ne):
        n_rounds = jnp.maximum(n_rounds, jnp.where(cnts[e] > 0, (ends[e] - a0[e] + rnd - 1) // rnd, 0))
    h = h_ref[...]
    gates = gate_ref[...]
    pos = pos_ref[...]
    row = lax.broadcasted_iota(I32, (rnd, TT), 0)
    grp_row = lax.broadcasted_iota(I32, (sub, xbuf.shape[1]), 0)

    def copy(e, r):
        dst = pl.ds(pl.multiple_of(a0[e] + r * rnd, sub), rnd)
        return pltpu.make_async_copy(xbuf.at[pl.ds(e * rnd, rnd)], xe_ref.at[e, dst], sems.at[e])

    def one_round(r):
        sel = jnp.concatenate(
            [jnp.where(pos[e:e + 1, :] - (a0[e] + r * rnd) == row, 1.0, 0.0).astype(BF16)
             for e in range(ne)], axis=0)
        xbuf[:, 0:d] = _dot(sel, h)
        xbuf[:, d:] = _dot(sel, gates)
        if r == 0:
            for e in range(ne):
                xbuf[e * rnd:e * rnd + sub, :] += carry[e]

        for e in range(ne):
            active = jnp.logical_and(cnts[e] > 0, a0[e] + r * rnd < ends[e])
            partial = jnp.logical_and(ends[e] > last_grp[e], (last_grp[e] - a0[e]) // rnd == r)

            @pl.when(jnp.logical_and(cnts[e] > 0, partial))
            def _():
                off = pl.multiple_of(e * rnd + (last_grp[e] - a0[e]) - r * rnd, sub)
                carry[e] = jnp.where(grp_row < ends[e] - last_grp[e], xbuf[pl.ds(off, sub), :], 0.0)

            @pl.when(active)
            def _():
                copy(e, r).start()

        for e in range(ne):
            @pl.when(jnp.logical_and(cnts[e] > 0, a0[e] + r * rnd < ends[e]))
            def _():
                copy(e, r).wait()

    for r in range((TT + sub - 1 + rnd - 1) // rnd):
        pl.when(r < n_rounds)(functools.partial(one_round, r))
    for e in range(ne):
        @pl.when(jnp.logical_and(cnts[e] > 0, ends[e] == last_grp[e]))
        def _():
            carry[e] = jnp.zeros((sub, xbuf.shape[1]), F32)


def _dispatch(starts, cnts, h2, pos, gate3, slot_rows):
    b, s, d = h2.shape
    nt = s // TT
    lat_tiles = nt - 1
    n_lat_steps = b * lat_tiles
    steps = n_lat_steps + b
    decode = functools.partial(_step_to_tile, n_lat_steps=n_lat_steps, lat_tiles=lat_tiles)

    def tile(step, *_):
        bi, t = decode(step)
        return bi, t, 0

    def pos_tile(step, *_):
        bi, t = decode(step)
        return bi, 0, t

    rows = N_EXPERTS * DISPATCH_ROUND
    width = d + LANES
    return pl.pallas_call(
        functools.partial(_dispatch_kernel, n_lat_steps=n_lat_steps, lat_tiles=lat_tiles, nt=nt, d=d,
                          slot_rows=slot_rows),
        grid_spec=pltpu.PrefetchScalarGridSpec(
            num_scalar_prefetch=2,
            grid=(steps,),
            in_specs=[pl.BlockSpec((None, TT, d), tile),
                      pl.BlockSpec((None, N_EXPERTS, TT), pos_tile),
                      pl.BlockSpec((None, TT, LANES), tile)],
            out_specs=pl.BlockSpec(memory_space=pl.ANY),
            scratch_shapes=[pltpu.VMEM((rows, width), F32),
                            pltpu.VMEM((N_EXPERTS, SUBLANES, width), F32),
                            pltpu.SemaphoreType.DMA((N_EXPERTS,))]),
        out_shape=jax.ShapeDtypeStruct((N_EXPERTS, slot_rows + DISPATCH_ROUND, width), F32),
        compiler_params=_cparams(1),
        name="dispatch",
    )(starts, cnts, h2, pos, gate3)


def _ffn_kernel(x_ref, wg_ref, wu_ref, wd_ref, y_ref, *, d):
    e = pl.program_id(0)
    x = x_ref[:, 0:d].astype(BF16)
    gt = _dot(x, wg_ref[...])
    up = _dot(x, wu_ref[...])
    hid = (gt * jax.nn.sigmoid(gt) * up).astype(BF16)
    y = _dot(hid, wd_ref[...])
    g3 = x_ref[:, d:]
    lane = lax.broadcasted_iota(I32, g3.shape, 1)
    mine = jnp.logical_and(lane % N_EXPERTS == e, lane < 3 * N_EXPERTS)
    gate = jnp.sum(jnp.where(mine, g3, 0.0), axis=1, keepdims=True)
    y_ref[...] = (y * gate).astype(y_ref.dtype)


def _ffn_tile(rows):
    best = BF16_ROWS
    for m in range(BF16_ROWS, 641, BF16_ROWS):
        if rows % m == 0:
            best = m
    return best


def _ffn(xe, wg, wu, wd, slot_rows):
    ne, _, width = xe.shape
    d, hdim = wg.shape[-2:]
    mt = _ffn_tile(slot_rows)
    wspec = lambda shape: pl.BlockSpec((None,) + shape, lambda e, j: (e, 0, 0))
    return pl.pallas_call(
        functools.partial(_ffn_kernel, d=d),
        grid=(ne, slot_rows // mt),
        in_specs=[pl.BlockSpec((None, mt, width), lambda e, j: (e, j, 0)),
                  wspec((d, hdim)), wspec((d, hdim)), wspec((hdim, d))],
        out_specs=pl.BlockSpec((None, mt, d), lambda e, j: (e, j, 0)),
        out_shape=jax.ShapeDtypeStruct((ne, slot_rows, d), BF16),
        compiler_params=_cparams(2),
        name="ffn",
    )(xe, wg, wu, wd)


def _combine_kernel(start_sm, cnt_sm, x_ref, post_ref, mod_ref, fg_ref, y_ref, o_ref, ybuf, acc_ref, sems,
                    *, d, nt, slot_rows, final):
    bi = pl.program_id(0)
    t = pl.program_id(1) + (1 if final else 0)
    ne = N_EXPERTS
    win = COMBINE_WINDOW
    sbase = (bi * nt + t) * ne
    starts = [start_sm[sbase + e] for e in range(ne)]
    cmax = cnt_sm[sbase]
    for e in range(1, ne):
        cmax = jnp.maximum(cmax, cnt_sm[sbase + e])
    post = post_ref[...]
    lane = lax.broadcasted_iota(I32, (TT, win), 1)

    def one_round(r):
        a = [jnp.minimum((starts[e] // BF16_ROWS) * BF16_ROWS + r * COMBINE_ROUND, slot_rows - win)
             for e in range(ne)]
        cps = [pltpu.make_async_copy(y_ref.at[e, pl.ds(pl.multiple_of(a[e], BF16_ROWS), win)],
                                     ybuf.at[pl.ds(e * win, win)], sems.at[e]) for e in range(ne)]
        for cp in cps:
            cp.start()
        sel = []
        for e in range(ne):
            rel = post[:, e:e + 1] - starts[e]
            in_round = jnp.logical_and(rel >= r * COMBINE_ROUND, rel < (r + 1) * COMBINE_ROUND)
            hit = jnp.logical_and(in_round, post[:, e:e + 1] - a[e] == lane)
            sel.append(jnp.where(hit, 1.0, 0.0).astype(BF16))
        sel = jnp.concatenate(sel, axis=1)
        for cp in cps:
            cp.wait()
        acc_ref[...] += _dot(sel, ybuf[...])

    n_rounds = (cmax + COMBINE_ROUND - 1) // COMBINE_ROUND
    acc_ref[...] = jnp.zeros_like(acc_ref)
    for r in range((TT + COMBINE_ROUND - 1) // COMBINE_ROUND):
        pl.when(r < n_rounds)(functools.partial(one_round, r))
    x2 = x_ref[...] + mod_ref[:, 5 * d:6 * d] * acc_ref[...]
    if final:
        x2 = x2 * lax.rsqrt(jnp.mean(x2 * x2, axis=-1, keepdims=True) + EPS) * fg_ref[...]
    o_ref[...] = x2


def _combine(starts, cnts, x1, pos_t, modsel, final_g, y, slot_rows, final):
    b, s, d = x1.shape
    nt = s // TT
    t0 = 1 if final else 0
    tile = lambda bi, t, *_: (bi, t + t0, 0)
    out_rows = s - t0 * TT if final else s
    out_tile = (lambda bi, t, *_: (bi, t, 0)) if final else tile
    return pl.pallas_call(
        functools.partial(_combine_kernel, d=d, nt=nt, slot_rows=slot_rows, final=final),
        grid_spec=pltpu.PrefetchScalarGridSpec(
            num_scalar_prefetch=2,
            grid=(b, nt - t0),
            in_specs=[pl.BlockSpec((None, TT, d), tile),
                      pl.BlockSpec((None, TT, LANES), tile),
                      pl.BlockSpec((None, None, 1, 6 * d), lambda bi, t, *_: (bi, jnp.minimum(t + t0, 1), 0, 0)),
                      pl.BlockSpec((1, d), lambda bi, t, *_: (0, 0)),
                      pl.BlockSpec(memory_space=pl.ANY)],
            out_specs=pl.BlockSpec((None, TT, d), out_tile),
            scratch_shapes=[pltpu.VMEM((N_EXPERTS * COMBINE_WINDOW, d), BF16),
                            pltpu.VMEM((TT, d), F32),
                            pltpu.SemaphoreType.DMA((N_EXPERTS,))]),
        out_shape=jax.ShapeDtypeStruct((b, out_rows, d), F32),
        compiler_params=_cparams(2),
        name="combine",
    )(starts, cnts, x1, pos_t, modsel, final_g, y)


def _rope_tables(n, ctx):
    tkn = np.arange(n)
    half = HEAD_DIM // 4
    inv = ROPE_THETA ** (-np.arange(half, dtype=np.float64) / half)
    ang_row = (tkn // GRID_W)[:, None] * inv[None, :]
    ang_col = (tkn % GRID_W)[:, None] * inv[None, :]
    ang = np.concatenate([ang_row, ang_row, ang_col, ang_col], axis=1)
    sign = np.tile(np.concatenate([-np.ones(half), np.ones(half)]), 2)
    cos = np.concatenate([np.ones((ctx, HEAD_DIM)), np.cos(ang)], axis=0)
    sin = np.concatenate([np.zeros((ctx, HEAD_DIM)), np.sin(ang) * sign[None, :]], axis=0)
    reps = LANES // HEAD_DIM
    return (jnp.asarray(np.tile(cos, (1, reps)), F32), jnp.asarray(np.tile(sin, (1, reps)), F32))


def _block_diag_ones(width):
    idx = np.arange(width) // HEAD_DIM
    return jnp.asarray(idx[:, None] == idx[None, :], BF16)


def kernel(x, c, ctx, c_ctx, w_ada, b_ada, norm1_g, norm2_g, w_in, q_norm_g, k_norm_g, na_rpb, conv_w, conv_b,
           conv_ln_g, conv_ln_b, w_out, w_router, w_gate, w_up, w_down, final_norm_g):
    b, n, d = x.shape
    nctx = ctx.shape[1]
    depth = w_ada.shape[0]
    assert nctx == TT and n % TT == 0 and n % GRID_W == 0 and b <= SUBLANES - 1
    assert w_router.shape[-1] == N_EXPERTS and w_in.shape[-1] == IN_WIDTH
    s = nctx + n
    nt = s // TT
    cap_lat = CAPACITY_FACTOR * n // N_EXPERTS
    cap_ctx = CAPACITY_FACTOR * nctx // N_EXPERTS
    assert cap_ctx <= DISPATCH_ROUND and cap_lat % BF16_ROWS == 0
    lat_rows = b * cap_lat
    ctx_rows = b * DISPATCH_ROUND

    cond = jnp.zeros((SUBLANES, d), F32).at[:b].set(c).at[b].set(c_ctx)
    mods = _ada(cond, w_ada, b_ada)
    cos, sin = _rope_tables(n, nctx)
    ones_q = _block_diag_ones(A_WIDTH)
    ones_k = _block_diag_ones(A_KV_WIDTH)
    tri = jnp.asarray(np.triu(np.ones((TT, TT))), BF16)

    stream = jnp.concatenate([ctx, x], axis=1)
    slot_rows = lat_rows + ctx_rows
    segments = ((0, 1, cap_ctx, lat_rows, DISPATCH_ROUND), (1, nt - 1, cap_lat, 0, cap_lat))
    bias_rows = n // GRID_W
    for i in range(depth):
        m = mods[i]
        modsel = jnp.stack([jnp.broadcast_to(m[b], (b, 6 * d)), m[:b]], axis=1)[:, :, None, :]
        qg = jnp.tile(q_norm_g[i], A_HEADS)[None, :]
        kg = jnp.tile(k_norm_g[i], A_KV_HEADS)[None, :]
        qa, ka, va, qb, kb, vb, u = _in_proj(stream, modsel, norm1_g[i][None, :], w_in[i].astype(BF16),
                                             cos, sin, qg, kg, ones_q, ones_k)
        a_mix = _attn_a(qa, ka, va, nctx)
        b_mix = _attn_b(qb, kb, vb, _na_bias_tables(na_rpb[i], bias_rows), nctx)
        wr = jnp.zeros((d, LANES), F32).at[:, :N_EXPERTS].set(w_router[i])
        wr_hi = wr.astype(BF16)
        wr_lo = (wr - wr_hi.astype(F32)).astype(BF16)
        x1, h2, aff_t, gate3 = _mix_out(a_mix, b_mix, u, stream, modsel, conv_w[i], conv_b[i][None, :],
                                        conv_ln_g[i][None, :], conv_ln_b[i][None, :], w_out[i].astype(BF16),
                                        norm2_g[i][None, :], wr_hi, wr_lo)
        pos, pos_t, start_t, cnt_t = _route(aff_t, tri, segments)
        starts = jnp.transpose(start_t, (0, 2, 1))[:, :nt, :].reshape(-1)
        cnts = jnp.transpose(cnt_t, (0, 2, 1))[:, :nt, :].reshape(-1)
        xe = _dispatch(starts, cnts, h2, pos, gate3, slot_rows)
        y = _ffn(xe, w_gate[i].astype(BF16), w_up[i].astype(BF16), w_down[i].astype(BF16), slot_rows)
        stream = _combine(starts, cnts, x1, pos_t, modsel, final_norm_g[None, :], y, slot_rows, i == depth - 1)
    return stream
```

```python
import functools

import numpy as np
import jax
import jax.numpy as jnp
from jax import lax
from jax.experimental import pallas as pl
from jax.experimental.pallas import tpu as pltpu

F32 = jnp.float32
BF16 = jnp.bfloat16
I32 = jnp.int32

GRID_W = 64
HEAD_DIM = 64
A_HEADS = 8
A_KV_HEADS = 2
A_GROUP = A_HEADS // A_KV_HEADS
B_HEADS = 4
C_CHANNELS = 256
CONV_WIDTH = 31
NA_WIN_ROWS = 8
NA_WIN_COLS = 16
N_EXPERTS = 16
CAPACITY_FACTOR = 2
ROPE_THETA = 10000.0
EPS = 1e-6
A_WIDTH = A_HEADS * HEAD_DIM
A_KV_WIDTH = A_KV_HEADS * HEAD_DIM
B_WIDTH = B_HEADS * HEAD_DIM
MIX_WIDTH = A_WIDTH + B_WIDTH + C_CHANNELS
IN_WIDTH = A_WIDTH + 2 * A_KV_WIDTH + 3 * B_WIDTH + 2 * C_CHANNELS

LANES = 128
SUBLANES = 8
BF16_ROWS = 16
VMEM_LIMIT = 56 * 1024 * 1024

TT = 256
NEG = -1e30
LOG2_E = float(np.log2(np.e))
CONV_HALO = 16
DISPATCH_ROUND = 64
COMBINE_WINDOW = 64
COMBINE_ROUND = COMBINE_WINDOW - BF16_ROWS


def _cparams(n_axes):
    return pltpu.CompilerParams(dimension_semantics=("arbitrary",) * n_axes, vmem_limit_bytes=VMEM_LIMIT)


def _split3(x):
    hi = x.astype(BF16)
    r1 = x - hi.astype(F32)
    mid = r1.astype(BF16)
    lo = (r1 - mid.astype(F32)).astype(BF16)
    return hi, mid, lo


def _dot(a, b):
    return jnp.dot(a, b, preferred_element_type=F32)


def _dot_nt(a, b):
    return lax.dot_general(a, b, (((1,), (1,)), ((), ())), preferred_element_type=F32)


def _dot3(a, b_hi, b_lo):
    a_hi = a.astype(BF16)
    a_lo = (a - a_hi.astype(F32)).astype(BF16)
    return _dot(a_hi, b_hi) + (_dot(a_lo, b_hi) + _dot(a_hi, b_lo))


def _ada_kernel(cond_ref, w_ref, b_ref, o_ref):
    c = cond_ref[...]
    s = c * jax.nn.sigmoid(c)
    w = w_ref[0]
    w_hi = w.astype(BF16)
    w_lo = (w - w_hi.astype(F32)).astype(BF16)
    o_ref[0] = _dot3(s, w_hi, w_lo) + b_ref[0]


def _ada(cond, w_ada, b_ada):
    depth, d, six_d = w_ada.shape
    bn = six_d // 4
    return pl.pallas_call(
        _ada_kernel,
        grid=(depth, six_d // bn),
        in_specs=[pl.BlockSpec((SUBLANES, d), lambda l, j: (0, 0)),
                  pl.BlockSpec((1, d, bn), lambda l, j: (l, 0, j)),
                  pl.BlockSpec((1, 1, bn), lambda l, j: (l, 0, j))],
        out_specs=pl.BlockSpec((1, SUBLANES, bn), lambda l, j: (l, 0, j)),
        out_shape=jax.ShapeDtypeStruct((depth, SUBLANES, six_d), F32),
        compiler_params=_cparams(2),
        name="ada",
    )(cond, w_ada, b_ada.reshape(depth, 1, six_d))


def _head_rms(x, ones_bd, g):
    sq = x * x
    hi = sq.astype(BF16)
    lo = (sq - hi.astype(F32)).astype(BF16)
    ssum = _dot(hi, ones_bd) + _dot(lo, ones_bd)
    return x * lax.rsqrt(ssum * (1.0 / HEAD_DIM) + EPS) * g


def _rope(x, cos, sin_signed):
    w = x.shape[-1]
    lane = lax.broadcasted_iota(I32, x.shape, 1)
    first_half = (lane % 32) < 16
    partner = jnp.where(first_half, pltpu.roll(x, w - 16, 1), pltpu.roll(x, 16, 1))
    return x * cos + partner * sin_signed


def _in_proj_kernel(x_ref, mod_ref, g1_ref, w_ref, cos_ref, sin_ref, qg_ref, kg_ref, onesq_ref, onesk_ref,
                    qa_ref, ka_ref, va_ref, qb_ref, kb_ref, vb_ref, u_ref, *, d):
    x = x_ref[...]
    y = x * lax.rsqrt(jnp.mean(x * x, axis=-1, keepdims=True) + EPS) * g1_ref[...]
    shift = mod_ref[:, 0:d]
    scale = mod_ref[:, d:2 * d]
    h = (y * (1.0 + scale) + shift).astype(BF16)
    p = _dot(h, w_ref[...])
    o = 0
    qa = p[:, o:o + A_WIDTH]; o += A_WIDTH
    ka = p[:, o:o + A_KV_WIDTH]; o += A_KV_WIDTH
    va = p[:, o:o + A_KV_WIDTH]; o += A_KV_WIDTH
    qb = p[:, o:o + B_WIDTH]; o += B_WIDTH
    kb = p[:, o:o + B_WIDTH]; o += B_WIDTH
    vb = p[:, o:o + B_WIDTH]; o += B_WIDTH
    cv = p[:, o:o + C_CHANNELS]; o += C_CHANNELS
    cg = p[:, o:o + C_CHANNELS]

    cos = cos_ref[...]
    sin = sin_ref[...]
    sm_scale = HEAD_DIM ** -0.5
    qa = _head_rms(qa, onesq_ref[...], qg_ref[...])
    qa = _rope(qa, jnp.concatenate([cos] * (A_WIDTH // LANES), axis=1),
               jnp.concatenate([sin] * (A_WIDTH // LANES), axis=1)) * (sm_scale * LOG2_E)
    ka = _head_rms(ka, onesk_ref[...], kg_ref[...])
    ka = _rope(ka, cos, sin)
    qa_ref[...] = qa.T.reshape(A_HEADS, HEAD_DIM, -1).astype(BF16)
    va_ref[...] = va.T.reshape(A_KV_HEADS, HEAD_DIM, -1).astype(BF16)
    for hh in range(A_KV_HEADS):
        ka_ref[hh] = ka[:, hh * HEAD_DIM:(hh + 1) * HEAD_DIM].astype(BF16)
    qb_ref[...] = (qb * sm_scale).astype(BF16)
    kb_ref[...] = kb.astype(BF16)
    vb_ref[...] = vb.astype(BF16)
    u_ref[...] = cv * jax.nn.sigmoid(cg)


def _in_proj(x, modsel, g1, w_in, cos, sin, qg, kg, ones_q, ones_k):
    b, s, d = x.shape
    nt = s // TT
    tile = lambda bi, t: (bi, t, 0)
    head_tile = lambda bi, t: (bi, 0, t, 0)
    head_tile_t = lambda bi, t: (bi, 0, 0, t)
    const2 = lambda bi, t: (0, 0)
    return pl.pallas_call(
        functools.partial(_in_proj_kernel, d=d),
        grid=(b, nt),
        in_specs=[pl.BlockSpec((None, TT, d), tile),
                  pl.BlockSpec((None, None, 1, 6 * d), lambda bi, t: (bi, jnp.minimum(t, 1), 0, 0)),
                  pl.BlockSpec((1, d), const2),
                  pl.BlockSpec((d, IN_WIDTH), const2),
                  pl.BlockSpec((TT, LANES), lambda bi, t: (t, 0)),
                  pl.BlockSpec((TT, LANES), lambda bi, t: (t, 0)),
                  pl.BlockSpec((1, A_WIDTH), const2),
                  pl.BlockSpec((1, A_KV_WIDTH), const2),
                  pl.BlockSpec((A_WIDTH, A_WIDTH), const2),
                  pl.BlockSpec((A_KV_WIDTH, A_KV_WIDTH), const2)],
        out_specs=[pl.BlockSpec((None, A_HEADS, HEAD_DIM, TT), head_tile_t),
                   pl.BlockSpec((None, A_KV_HEADS, TT, HEAD_DIM), head_tile),
                   pl.BlockSpec((None, A_KV_HEADS, HEAD_DIM, TT), head_tile_t),
                   pl.BlockSpec((None, TT, B_WIDTH), tile),
                   pl.BlockSpec((None, TT, B_WIDTH), tile),
                   pl.BlockSpec((None, TT, B_WIDTH), tile),
                   pl.BlockSpec((None, TT, C_CHANNELS), tile)],
        out_shape=[jax.ShapeDtypeStruct((b, A_HEADS, HEAD_DIM, s), BF16),
                   jax.ShapeDtypeStruct((b, A_KV_HEADS, s, HEAD_DIM), BF16),
                   jax.ShapeDtypeStruct((b, A_KV_HEADS, HEAD_DIM, s), BF16),
                   jax.ShapeDtypeStruct((b, s, B_WIDTH), BF16),
                   jax.ShapeDtypeStruct((b, s, B_WIDTH), BF16),
                   jax.ShapeDtypeStruct((b, s, B_WIDTH), BF16),
                   jax.ShapeDtypeStruct((b, s, C_CHANNELS), F32)],
        compiler_params=_cparams(2),
        name="in_proj",
    )(x, modsel, g1, w_in, cos, sin, qg, kg, ones_q, ones_k)


def _attn_a_kernel(q_ref, k_ref, v_ref, o_ref, *, ctx, tk, n_chunks):
    t = pl.program_id(2)
    n_heads, dh, tq = q_ref.shape
    g = n_heads // 2
    qs = [jnp.concatenate([q_ref[2 * i], q_ref[2 * i + 1]], axis=1) for i in range(g)]

    def update(state, s, v):
        m, l, acc = state
        m_new = jnp.maximum(m, jnp.max(s, axis=0, keepdims=True))
        alpha = jnp.exp2(m - m_new)
        p = jnp.exp2(s - m_new)
        l = alpha * l + jnp.sum(p, axis=0, keepdims=True)
        return m_new, l, alpha * acc + _dot(v, p.astype(BF16))

    def step(s0, k, v, k_next, states):
        out = []
        s = s0
        for i in range(g):
            s_next = _dot(k, qs[i + 1]) if i + 1 < g else _dot(k_next, qs[0])
            out.append(update(states[i], s, v))
            s = s_next
        return s, tuple(out)

    def chunk(c):
        return pl.ds(pl.multiple_of(ctx + c * tk, int(np.gcd(ctx, tk))), tk)

    init = tuple((jnp.full((1, 2 * tq), NEG, F32), jnp.zeros((1, 2 * tq), F32), jnp.zeros((dh, 2 * tq), F32))
                 for _ in range(g))
    k_ctx = k_ref[0:ctx, :]
    v_ctx = v_ref[:, 0:ctx]
    s0 = _dot(k_ctx, qs[0])

    def finish(states):
        o_t = jnp.concatenate([part for (_, l, acc) in states
                               for part in ((acc / l)[:, 0:tq], (acc / l)[:, tq:2 * tq])], axis=0)
        o_ref[...] = o_t.T.astype(o_ref.dtype)

    @pl.when(t == 0)
    def _():
        _, states = step(s0, k_ctx, v_ctx, k_ctx, init)
        finish(states)

    @pl.when(t > 0)
    def _():
        carry = step(s0, k_ctx, v_ctx, k_ref[chunk(0), :], init)

        def body(c, carry):
            s_pair0, states = carry
            nxt = jnp.minimum(c + 1, n_chunks - 1)
            return step(s_pair0, k_ref[chunk(c), :], v_ref[:, chunk(c)], k_ref[chunk(nxt), :], states)

        finish(lax.fori_loop(0, n_chunks, body, carry, unroll=True)[1])


def _attn_a(qa_t, ka, va_t, ctx):
    b, _, dh, s = qa_t.shape
    nt = s // TT
    tk = 1024 if (s - ctx) % 1024 == 0 else TT
    return pl.pallas_call(
        functools.partial(_attn_a_kernel, ctx=ctx, tk=tk, n_chunks=(s - ctx) // tk),
        grid=(b, A_KV_HEADS, nt),
        in_specs=[pl.BlockSpec((None, A_GROUP, dh, TT), lambda bi, j, t: (bi, j, 0, t)),
                  pl.BlockSpec((None, None, s, dh), lambda bi, j, t: (bi, j, 0, 0)),
                  pl.BlockSpec((None, None, dh, s), lambda bi, j, t: (bi, j, 0, 0))],
        out_specs=pl.BlockSpec((None, TT, A_GROUP * dh), lambda bi, j, t: (bi, t, j)),
        out_shape=jax.ShapeDtypeStruct((b, s, A_WIDTH), BF16),
        compiler_params=_cparams(3),
        name="attn_a",
    )(qa_t, ka, va_t)


def _attn_b_kernel(q_ref, k_ref, v_ref, bias_ref, o_ref, *, ctx, rows, wr):
    t = pl.program_id(1)
    tt, w = q_ref.shape
    rows_per_tile = tt // GRID_W
    win = wr * GRID_W
    q = q_ref[...]
    kc = k_ref[0:ctx, :]
    vc = v_ref[0:ctx, :]
    lane_head = lax.broadcasted_iota(I32, (tt, w), 1) // HEAD_DIM

    def softmax_parts(parts):
        m = parts[0].max(axis=1, keepdims=True)
        for s in parts[1:]:
            m = jnp.maximum(m, s.max(axis=1, keepdims=True))
        es = [jnp.exp(s - m) for s in parts]
        tot = es[0].sum(axis=1, keepdims=True)
        for e in es[1:]:
            tot = tot + e.sum(axis=1, keepdims=True)
        inv = 1.0 / tot
        return [(e * inv).astype(BF16) for e in es]

    @pl.when(t == 0)
    def _():
        out = jnp.zeros((tt, w), F32)
        for hh in range(B_HEADS):
            qh = jnp.where(lane_head == hh, q, jnp.zeros_like(q))
            (p,) = softmax_parts([_dot_nt(qh, kc)])
            out = out + jnp.where(lane_head == hh, _dot(p, vc), 0.0)
        o_ref[...] = out.astype(o_ref.dtype)

    @pl.when(t > 0)
    def _():
        r0 = (t - 1) * rows_per_tile
        k_rows, v_rows, variant = [], [], []
        for i in range(rows_per_tile):
            r = r0 + i
            rs = jnp.clip(r - wr // 2, 0, rows - wr)
            start = pl.multiple_of(ctx + rs * GRID_W, GRID_W)
            k_rows.append(k_ref[pl.ds(start, win), :])
            v_rows.append(v_ref[pl.ds(start, win), :])
            variant.append(r - rs)
        scores = []
        for hh in range(B_HEADS):
            qh = jnp.where(lane_head == hh, q, jnp.zeros_like(q))
            s_nb = jnp.concatenate([_dot_nt(qh[i * GRID_W:(i + 1) * GRID_W], k_rows[i]) + bias_ref[variant[i], hh]
                                    for i in range(rows_per_tile)], axis=0)
            scores.append([_dot_nt(qh, kc), s_nb])
        probs = [softmax_parts(sc) for sc in scores]
        out = jnp.zeros((tt, w), F32)
        for hh in range(B_HEADS):
            p_cx, p_nb = probs[hh]
            o_nb = jnp.concatenate([_dot(p_nb[i * GRID_W:(i + 1) * GRID_W], v_rows[i])
                                    for i in range(rows_per_tile)], axis=0)
            out = out + jnp.where(lane_head == hh, _dot(p_cx, vc) + o_nb, 0.0)
        o_ref[...] = out.astype(o_ref.dtype)


def _attn_b(qb, kb, vb, bias, ctx):
    b, s, w = qb.shape
    nt = s // TT
    rows = (s - ctx) // GRID_W
    wr = min(NA_WIN_ROWS, rows)
    tile = lambda bi, t: (bi, t, 0)
    whole = lambda bi, t: (bi, 0, 0)
    return pl.pallas_call(
        functools.partial(_attn_b_kernel, ctx=ctx, rows=rows, wr=wr),
        grid=(b, nt),
        in_specs=[pl.BlockSpec((None, TT, w), tile),
                  pl.BlockSpec((None, s, w), whole),
                  pl.BlockSpec((None, s, w), whole),
                  pl.BlockSpec(bias.shape, lambda bi, t: (0, 0, 0, 0))],
        out_specs=pl.BlockSpec((None, TT, w), tile),
        out_shape=jax.ShapeDtypeStruct((b, s, w), BF16),
        compiler_params=_cparams(2),
        name="attn_b",
    )(qb, kb, vb, bias)


def _na_bias_tables(rpb, rows):
    wr = min(NA_WIN_ROWS, rows)
    wc = NA_WIN_COLS
    q = np.arange(GRID_W)
    cs = np.clip(q - wc // 2, 0, GRID_W - wc)
    kc = np.arange(GRID_W)
    valid = (kc[None, :] >= cs[:, None]) & (kc[None, :] < cs[:, None] + wc)
    dc = np.clip(kc[None, :] - q[:, None] + (NA_WIN_COLS - 1), 0, 2 * NA_WIN_COLS - 2)
    v = np.arange(wr)
    wrow = np.arange(wr)
    dr = wrow[None, :] - v[:, None] + (NA_WIN_ROWS - 1)
    pick_r = jnp.asarray(dr[None, :, :] == np.arange(2 * NA_WIN_ROWS - 1)[:, None, None], F32)
    pick_c = jnp.asarray((dc[None, :, :] == np.arange(2 * NA_WIN_COLS - 1)[:, None, None])
                         & valid[None, :, :], F32)
    tab = jnp.einsum("hrc,rvw,cqk->vhqwk", rpb.astype(F32), pick_r, pick_c, precision=lax.Precision.HIGHEST)
    tab = tab + jnp.asarray(np.where(valid, 0.0, NEG), F32)[None, None, :, None, :]
    return tab.reshape(wr, B_HEADS, GRID_W, wr * GRID_W)


def _mix_out_kernel(a_ref, b_ref, up_ref, uc_ref, un_ref, x_ref, mod_ref, cw_ref, cb_ref, lg_ref, lb_ref,
                    wo_ref, g2_ref, wrh_ref, wrl_ref,
                    x1_ref, h2_ref, aff_ref, gate_ref, *, d, nt):
    t = pl.program_id(1)
    tt = uc_ref.shape[0]
    prev_ok = t >= 2
    next_ok = jnp.logical_and(t >= 1, t < nt - 1)
    prev = jnp.where(prev_ok, up_ref[tt - CONV_HALO:tt, :], 0.0)
    nxt = jnp.where(next_ok, un_ref[0:CONV_HALO, :], 0.0)
    win = jnp.concatenate([prev, uc_ref[...], nxt], axis=0)
    off = CONV_HALO - CONV_WIDTH // 2
    acc = jnp.zeros((tt, C_CHANNELS), F32) + cb_ref[...]
    for k in range(CONV_WIDTH):
        acc = acc + win[off + k:off + k + tt, :] * cw_ref[k:k + 1, :]
    mu = jnp.mean(acc, axis=-1, keepdims=True)
    var = jnp.mean(jnp.square(acc - mu), axis=-1, keepdims=True)
    yc = (acc - mu) * lax.rsqrt(var + EPS) * lg_ref[...] + lb_ref[...]
    c = (yc * jax.nn.sigmoid(yc)).astype(BF16)

    mix = (_dot(a_ref[...], wo_ref[0:A_WIDTH, :])
           + _dot(b_ref[...], wo_ref[A_WIDTH:A_WIDTH + B_WIDTH, :])
           + _dot(c, wo_ref[A_WIDTH + B_WIDTH:MIX_WIDTH, :]))
    g1 = mod_ref[:, 2 * d:3 * d]
    x1 = x_ref[...] + g1 * mix
    x1_ref[...] = x1
    y = x1 * lax.rsqrt(jnp.mean(x1 * x1, axis=-1, keepdims=True) + EPS) * g2_ref[...]
    h2 = y * (1.0 + mod_ref[:, 4 * d:5 * d]) + mod_ref[:, 3 * d:4 * d]
    h2_ref[...] = h2.astype(BF16)
    logits = _dot3(h2, wrh_ref[...], wrl_ref[...])
    lane = lax.broadcasted_iota(I32, logits.shape, 1)
    logits = jnp.where(lane < N_EXPERTS, logits, NEG)
    e = jnp.exp(logits - jnp.max(logits, axis=-1, keepdims=True))
    aff = e / jnp.sum(e, axis=-1, keepdims=True)
    aff_ref[...] = aff.T[0:N_EXPERTS, :]
    hi, mid, lo = _split3(aff)
    gate = jnp.where(lane < N_EXPERTS, hi.astype(F32),
                     jnp.where(lane < 2 * N_EXPERTS, pltpu.roll(mid.astype(F32), N_EXPERTS, 1),
                               jnp.where(lane < 3 * N_EXPERTS, pltpu.roll(lo.astype(F32), 2 * N_EXPERTS, 1), 0.0)))
    gate_ref[...] = gate.astype(BF16)


def _mix_out(a, bmix, u, x, modsel, cw, cb, lg, lb, wo, g2, wr_hi, wr_lo):
    b, s, d = x.shape
    nt = s // TT
    tile = lambda bi, t: (bi, t, 0)
    const2 = lambda bi, t: (0, 0)
    return pl.pallas_call(
        functools.partial(_mix_out_kernel, d=d, nt=nt),
        grid=(b, nt),
        in_specs=[pl.BlockSpec((None, TT, A_WIDTH), tile),
                  pl.BlockSpec((None, TT, B_WIDTH), tile),
                  pl.BlockSpec((None, TT, C_CHANNELS), lambda bi, t: (bi, jnp.maximum(t - 1, 0), 0)),
                  pl.BlockSpec((None, TT, C_CHANNELS), tile),
                  pl.BlockSpec((None, TT, C_CHANNELS), lambda bi, t: (bi, jnp.minimum(t + 1, nt - 1), 0)),
                  pl.BlockSpec((None, TT, d), tile),
                  pl.BlockSpec((None, None, 1, 6 * d), lambda bi, t: (bi, jnp.minimum(t, 1), 0, 0)),
                  pl.BlockSpec((CONV_WIDTH, C_CHANNELS), const2),
                  pl.BlockSpec((1, C_CHANNELS), const2),
                  pl.BlockSpec((1, C_CHANNELS), const2),
                  pl.BlockSpec((1, C_CHANNELS), const2),
                  pl.BlockSpec((MIX_WIDTH, d), const2),
                  pl.BlockSpec((1, d), const2),
                  pl.BlockSpec((d, LANES), const2),
                  pl.BlockSpec((d, LANES), const2)],
        out_specs=[pl.BlockSpec((None, TT, d), tile),
                   pl.BlockSpec((None, TT, d), tile),
                   pl.BlockSpec((None, N_EXPERTS, TT), lambda bi, t: (bi, 0, t)),
                   pl.BlockSpec((None, TT, LANES), tile)],
        out_shape=[jax.ShapeDtypeStruct((b, s, d), F32),
                   jax.ShapeDtypeStruct((b, s, d), BF16),
                   jax.ShapeDtypeStruct((b, N_EXPERTS, s), F32),
                   jax.ShapeDtypeStruct((b, s, LANES), BF16)],
        compiler_params=_cparams(2),
        name="mix_out",
    )(a, bmix, u, u, u, x, modsel, cw, cb, lg, lb, wo, g2, wr_hi, wr_lo)


def _route_kernel(aff_ref, tri_ref, pos_ref, post_ref, start_ref, cnt_ref, *, segments):
    bi = pl.program_id(0)
    tri = tri_ref[...]
    ne = N_EXPERTS
    lane_t = lax.broadcasted_iota(I32, (ne, LANES), 1)
    starts = jnp.zeros((ne, LANES), F32)
    cnts = jnp.zeros((ne, LANES), F32)

    def block_cumsum(mask_f32, carry):
        inc = _dot(mask_f32.astype(BF16), tri) + carry
        return inc - mask_f32, inc[:, TT - 1:TT]

    for (tile0, ntiles, cap, row0, rows_per_sample) in segments:
        lo = tile0 * TT
        n = ntiles * TT
        bits = pltpu.bitcast(aff_ref[:, lo:lo + n], I32)

        def search(i, cur):
            cand = cur | lax.shift_left(jnp.int32(1), 30 - i)
            c = jnp.sum(jnp.where(bits >= cand, 1.0, 0.0), axis=1, keepdims=True)
            return jnp.where(c >= cap, cand, cur)

        thr = lax.fori_loop(0, 31, search, jnp.zeros((ne, 1), I32))
        n_gt = jnp.sum(jnp.where(bits > thr, 1.0, 0.0), axis=1, keepdims=True)
        need = cap - n_gt
        base = (row0 + bi * rows_per_sample).astype(F32)
        tie_carry = jnp.zeros((ne, 1), F32)
        pos_carry = jnp.zeros((ne, 1), F32)
        for k in range(ntiles):
            bk = bits[:, k * TT:(k + 1) * TT]
            eq = jnp.where(bk == thr, 1.0, 0.0)
            tie_rank, tie_carry = block_cumsum(eq, tie_carry)
            sel = jnp.where(bk > thr, 1.0, jnp.where(tie_rank < need, eq, 0.0))
            rank, new_carry = block_cumsum(sel, pos_carry)
            pos = jnp.where(sel > 0.0, rank + base, -1.0)
            pos_ref[:, lo + k * TT:lo + (k + 1) * TT] = pos.astype(I32)
            padded = jnp.concatenate([pos, jnp.full((LANES - ne, TT), -1.0, F32)], axis=0)
            post_ref[lo + k * TT:lo + (k + 1) * TT, :] = padded.T.astype(I32)
            starts = jnp.where(lane_t == tile0 + k, pos_carry + base, starts)
            cnts = jnp.where(lane_t == tile0 + k, new_carry - pos_carry, cnts)
            pos_carry = new_carry
    start_ref[...] = starts.astype(I32)
    cnt_ref[...] = cnts.astype(I32)


def _route(aff_t, tri, segments):
    b, ne, s = aff_t.shape
    return pl.pallas_call(
        functools.partial(_route_kernel, segments=segments),
        grid=(b,),
        in_specs=[pl.BlockSpec((None, ne, s), lambda bi: (bi, 0, 0)),
                  pl.BlockSpec((TT, TT), lambda bi: (0, 0))],
        out_specs=[pl.BlockSpec((None, ne, s), lambda bi: (bi, 0, 0)),
                   pl.BlockSpec((None, s, LANES), lambda bi: (bi, 0, 0)),
                   pl.BlockSpec((None, ne, LANES), lambda bi: (bi, 0, 0)),
                   pl.BlockSpec((None, ne, LANES), lambda bi: (bi, 0, 0))],
        out_shape=[jax.ShapeDtypeStruct((b, ne, s), I32),
                   jax.ShapeDtypeStruct((b, s, LANES), I32),
                   jax.ShapeDtypeStruct((b, ne, LANES), I32),
                   jax.ShapeDtypeStruct((b, ne, LANES), I32)],
        compiler_params=_cparams(1),
        name="route",
    )(aff_t, tri)


def _step_to_tile(step, n_lat_steps, lat_tiles):
    is_lat = step < n_lat_steps
    bi = jnp.where(is_lat, step // lat_tiles, step - n_lat_steps)
    t = jnp.where(is_lat, 1 + step % lat_tiles, 0)
    return bi, t


def _dispatch_kernel(start_sm, cnt_sm, h_ref, pos_ref, gate_ref, xe_ref, xbuf, xextra, carry, sems,
                     *, n_lat_steps, lat_tiles, nt, d, slot_rows):
    step = pl.program_id(0)
    last_step = pl.num_programs(0) - 1
    slot = step % 2
    bi, t = _step_to_tile(step, n_lat_steps, lat_tiles)
    ne = N_EXPERTS
    rnd = DISPATCH_ROUND
    sub = SUBLANES
    width = xextra.shape[1]

    @pl.when(step == 0)
    def _():
        carry[...] = jnp.zeros_like(carry)
        xextra[0:rnd, :] = jnp.zeros((rnd, width), F32)
        pads = [pltpu.make_async_copy(xextra.at[pl.ds(0, rnd)], xe_ref.at[e, pl.ds(slot_rows, rnd)],
                                      sems.at[2, e]) for e in range(ne)]
        for cp in pads:
            cp.start()
        for cp in pads:
            cp.wait()

    sbase = (bi * nt + t) * ne
    starts = [start_sm[sbase + e] for e in range(ne)]
    cnts = [cnt_sm[sbase + e] for e in range(ne)]
    ends = [starts[e] + cnts[e] for e in range(ne)]
    a0 = [(starts[e] // sub) * sub for e in range(ne)]
    last_grp = [(ends[e] // sub) * sub for e in range(ne)]
    n_rounds = jnp.int32(0)
    for e in range(ne):
        n_rounds = jnp.maximum(n_rounds, jnp.where(cnts[e] > 0, (ends[e] - a0[e] + rnd - 1) // rnd, 0))
    h = h_ref[...]
    gates = gate_ref[...]
    pos = pos_ref[...]
    row = lax.broadcasted_iota(I32, (rnd, TT), 0)
    grp_row = lax.broadcasted_iota(I32, (sub, width), 0)

    def copy(e, r, buf, sem_row):
        dst = pl.ds(pl.multiple_of(a0[e] + r * rnd, sub), rnd)
        return pltpu.make_async_copy(buf.at[pl.ds(e * rnd, rnd)], xe_ref.at[e, dst], sems.at[sem_row, e])

    def active(e, r):
        return jnp.logical_and(cnts[e] > 0, a0[e] + r * rnd < ends[e])

    def fill(r, buf):
        sel = jnp.concatenate(
            [jnp.where(pos[e:e + 1, :] - (a0[e] + r * rnd) == row, 1.0, 0.0).astype(BF16)
             for e in range(ne)], axis=0)
        buf[:, 0:d] = _dot(sel, h)
        buf[:, d:] = _dot(sel, gates)
        if r == 0:
            for e in range(ne):
                buf[e * rnd:e * rnd + sub, :] += carry[e]
        for e in range(ne):
            partial = jnp.logical_and(ends[e] > last_grp[e], (last_grp[e] - a0[e]) // rnd == r)

            @pl.when(jnp.logical_and(cnts[e] > 0, partial))
            def _():
                off = pl.multiple_of(e * rnd + (last_grp[e] - a0[e]) - r * rnd, sub)
                carry[e] = jnp.where(grp_row < ends[e] - last_grp[e], buf[pl.ds(off, sub), :], 0.0)

    def later_round(r):
        fill(r, xextra)
        for e in range(ne):
            pl.when(active(e, r))(lambda e=e: copy(e, r, xextra, 2).start())
        for e in range(ne):
            pl.when(active(e, r))(lambda e=e: copy(e, r, xextra, 2).wait())

    pl.when(n_rounds > 0)(lambda: fill(0, xbuf.at[slot]))

    @pl.when(step > 0)
    def _():
        pbi, pt = _step_to_tile(jnp.maximum(step - 1, 0), n_lat_steps, lat_tiles)
        pbase = (pbi * nt + pt) * ne
        for e in range(ne):
            @pl.when(cnt_sm[pbase + e] > 0)
            def _():
                pltpu.make_async_copy(xbuf.at[1 - slot, pl.ds(e * rnd, rnd)], xe_ref.at[e, pl.ds(0, rnd)],
                                      sems.at[1 - slot, e]).wait()

    for e in range(ne):
        pl.when(cnts[e] > 0)(lambda e=e: copy(e, 0, xbuf.at[slot], slot).start())
    for r in range(1, (TT + sub - 1 + rnd - 1) // rnd):
        pl.when(r < n_rounds)(functools.partial(later_round, r))
    for e in range(ne):
        @pl.when(jnp.logical_and(cnts[e] > 0, ends[e] == last_grp[e]))
        def _():
            carry[e] = jnp.zeros((sub, width), F32)

    @pl.when(step == last_step)
    def _():
        for e in range(ne):
            pl.when(cnts[e] > 0)(lambda e=e: copy(e, 0, xbuf.at[slot], slot).wait())


def _dispatch(starts, cnts, h2, pos, gate3, slot_rows):
    b, s, d = h2.shape
    nt = s // TT
    lat_tiles = nt - 1
    n_lat_steps = b * lat_tiles
    steps = n_lat_steps + b
    decode = functools.partial(_step_to_tile, n_lat_steps=n_lat_steps, lat_tiles=lat_tiles)

    def tile(step, *_):
        bi, t = decode(step)
        return bi, t, 0

    def pos_tile(step, *_):
        bi, t = decode(step)
        return bi, 0, t

    rows = N_EXPERTS * DISPATCH_ROUND
    width = d + LANES
    return pl.pallas_call(
        functools.partial(_dispatch_kernel, n_lat_steps=n_lat_steps, lat_tiles=lat_tiles, nt=nt, d=d,
                          slot_rows=slot_rows),
        grid_spec=pltpu.PrefetchScalarGridSpec(
            num_scalar_prefetch=2,
            grid=(steps,),
            in_specs=[pl.BlockSpec((None, TT, d), tile),
                      pl.BlockSpec((None, N_EXPERTS, TT), pos_tile),
                      pl.BlockSpec((None, TT, LANES), tile)],
            out_specs=pl.BlockSpec(memory_space=pl.ANY),
            scratch_shapes=[pltpu.VMEM((2, rows, width), F32),
                            pltpu.VMEM((rows, width), F32),
                            pltpu.VMEM((N_EXPERTS, SUBLANES, width), F32),
                            pltpu.SemaphoreType.DMA((3, N_EXPERTS))]),
        out_shape=jax.ShapeDtypeStruct((N_EXPERTS, slot_rows + DISPATCH_ROUND, width), F32),
        compiler_params=_cparams(1),
        name="dispatch",
    )(starts, cnts, h2, pos, gate3)


def _ffn_kernel(x_ref, wg_ref, wu_ref, wd_ref, y_ref, *, d):
    e = pl.program_id(0)
    x = x_ref[:, 0:d].astype(BF16)
    gt = _dot(x, wg_ref[...])
    up = _dot(x, wu_ref[...])
    hid = (gt * jax.nn.sigmoid(gt) * up).astype(BF16)
    y = _dot(hid, wd_ref[...])
    g3 = x_ref[:, d:]
    lane = lax.broadcasted_iota(I32, g3.shape, 1)
    mine = jnp.logical_and(lane % N_EXPERTS == e, lane < 3 * N_EXPERTS)
    gate = jnp.sum(jnp.where(mine, g3, 0.0), axis=1, keepdims=True)
    y_ref[...] = (y * gate).astype(y_ref.dtype)


def _ffn_tile(rows):
    best = BF16_ROWS
    for m in range(BF16_ROWS, 641, BF16_ROWS):
        if rows % m == 0:
            best = m
    return best


def _ffn(xe, wg, wu, wd, slot_rows):
    ne, _, width = xe.shape
    d, hdim = wg.shape[-2:]
    mt = _ffn_tile(slot_rows)
    wspec = lambda shape: pl.BlockSpec((None,) + shape, lambda e, j: (e, 0, 0))
    return pl.pallas_call(
        functools.partial(_ffn_kernel, d=d),
        grid=(ne, slot_rows // mt),
        in_specs=[pl.BlockSpec((None, mt, width), lambda e, j: (e, j, 0)),
                  wspec((d, hdim)), wspec((d, hdim)), wspec((hdim, d))],
        out_specs=pl.BlockSpec((None, mt, d), lambda e, j: (e, j, 0)),
        out_shape=jax.ShapeDtypeStruct((ne, slot_rows, d), BF16),
        compiler_params=_cparams(2),
        name="ffn",
    )(xe, wg, wu, wd)


def _combine_kernel(start_sm, cnt_sm, x_ref, post_ref, mod_ref, fg_ref, y_ref, o_ref, ybuf, yextra, acc_ref, sems,
                    *, d, nt, slot_rows, final):
    t0 = 1 if final else 0
    bi = pl.program_id(0)
    tt = pl.program_id(1)
    tiles = pl.num_programs(1)
    lin = bi * tiles + tt
    slot = lin % 2
    ne = N_EXPERTS
    win = COMBINE_WINDOW
    sbase = (bi * nt + tt + t0) * ne
    starts = [start_sm[sbase + e] for e in range(ne)]
    cmax = cnt_sm[sbase]
    for e in range(1, ne):
        cmax = jnp.maximum(cmax, cnt_sm[sbase + e])
    post = post_ref[...]
    lane = lax.broadcasted_iota(I32, (TT, win), 1)

    def window_rows(first_slots, r):
        return [jnp.minimum((first_slots[e] // BF16_ROWS) * BF16_ROWS + r * COMBINE_ROUND, slot_rows - win)
                for e in range(ne)]

    def fetch(rows, dst, sem_row):
        return [pltpu.make_async_copy(y_ref.at[e, pl.ds(pl.multiple_of(rows[e], BF16_ROWS), win)],
                                      dst.at[pl.ds(e * win, win)], sems.at[sem_row, e]) for e in range(ne)]

    def add_round(r, rows, src):
        sel = []
        for e in range(ne):
            rel = post[:, e:e + 1] - starts[e]
            in_round = jnp.logical_and(rel >= r * COMBINE_ROUND, rel < (r + 1) * COMBINE_ROUND)
            hit = jnp.logical_and(in_round, post[:, e:e + 1] - rows[e] == lane)
            sel.append(jnp.where(hit, 1.0, 0.0).astype(BF16))
        acc_ref[...] += _dot(jnp.concatenate(sel, axis=1), src[...])

    rows0 = window_rows(starts, 0)

    @pl.when(lin == 0)
    def _():
        for cp in fetch(rows0, ybuf.at[slot], slot):
            cp.start()

    @pl.when(lin + 1 < pl.num_programs(0) * tiles)
    def _():
        nxt = lin + 1
        nbase = ((nxt // tiles) * nt + nxt % tiles + t0) * ne
        for cp in fetch(window_rows([start_sm[nbase + e] for e in range(ne)], 0), ybuf.at[1 - slot], 1 - slot):
            cp.start()

    acc_ref[...] = jnp.zeros_like(acc_ref)
    for cp in fetch(rows0, ybuf.at[slot], slot):
        cp.wait()
    add_round(0, rows0, ybuf.at[slot])

    def later_round(r):
        rows = window_rows(starts, r)
        cps = fetch(rows, yextra, 2)
        for cp in cps:
            cp.start()
        for cp in cps:
            cp.wait()
        add_round(r, rows, yextra)

    n_rounds = (cmax + COMBINE_ROUND - 1) // COMBINE_ROUND
    for r in range(1, (TT + COMBINE_ROUND - 1) // COMBINE_ROUND):
        pl.when(r < n_rounds)(functools.partial(later_round, r))
    x2 = x_ref[...] + mod_ref[:, 5 * d:6 * d] * acc_ref[...]
    if final:
        x2 = x2 * lax.rsqrt(jnp.mean(x2 * x2, axis=-1, keepdims=True) + EPS) * fg_ref[...]
    o_ref[...] = x2


def _combine(starts, cnts, x1, pos_t, modsel, final_g, y, slot_rows, final):
    b, s, d = x1.shape
    nt = s // TT
    t0 = 1 if final else 0
    tile = lambda bi, t, *_: (bi, t + t0, 0)
    out_rows = s - t0 * TT if final else s
    out_tile = (lambda bi, t, *_: (bi, t, 0)) if final else tile
    return pl.pallas_call(
        functools.partial(_combine_kernel, d=d, nt=nt, slot_rows=slot_rows, final=final),
        grid_spec=pltpu.PrefetchScalarGridSpec(
            num_scalar_prefetch=2,
            grid=(b, nt - t0),
            in_specs=[pl.BlockSpec((None, TT, d), tile),
                      pl.BlockSpec((None, TT, LANES), tile),
                      pl.BlockSpec((None, None, 1, 6 * d), lambda bi, t, *_: (bi, jnp.minimum(t + t0, 1), 0, 0)),
                      pl.BlockSpec((1, d), lambda bi, t, *_: (0, 0)),
                      pl.BlockSpec(memory_space=pl.ANY)],
            out_specs=pl.BlockSpec((None, TT, d), out_tile),
            scratch_shapes=[pltpu.VMEM((2, N_EXPERTS * COMBINE_WINDOW, d), BF16),
                            pltpu.VMEM((N_EXPERTS * COMBINE_WINDOW, d), BF16),
                            pltpu.VMEM((TT, d), F32),
                            pltpu.SemaphoreType.DMA((3, N_EXPERTS))]),
        out_shape=jax.ShapeDtypeStruct((b, out_rows, d), F32),
        compiler_params=_cparams(2),
        name="combine",
    )(starts, cnts, x1, pos_t, modsel, final_g, y)


def _rope_tables(n, ctx):
    tkn = np.arange(n)
    half = HEAD_DIM // 4
    inv = ROPE_THETA ** (-np.arange(half, dtype=np.float64) / half)
    ang_row = (tkn // GRID_W)[:, None] * inv[None, :]
    ang_col = (tkn % GRID_W)[:, None] * inv[None, :]
    ang = np.concatenate([ang_row, ang_row, ang_col, ang_col], axis=1)
    sign = np.tile(np.concatenate([-np.ones(half), np.ones(half)]), 2)
    cos = np.concatenate([np.ones((ctx, HEAD_DIM)), np.cos(ang)], axis=0)
    sin = np.concatenate([np.zeros((ctx, HEAD_DIM)), np.sin(ang) * sign[None, :]], axis=0)
    reps = LANES // HEAD_DIM
    return (jnp.asarray(np.tile(cos, (1, reps)), F32), jnp.asarray(np.tile(sin, (1, reps)), F32))


def _block_diag_ones(width):
    idx = np.arange(width) // HEAD_DIM
    return jnp.asarray(idx[:, None] == idx[None, :], BF16)


def kernel(x, c, ctx, c_ctx, w_ada, b_ada, norm1_g, norm2_g, w_in, q_norm_g, k_norm_g, na_rpb, conv_w, conv_b,
           conv_ln_g, conv_ln_b, w_out, w_router, w_gate, w_up, w_down, final_norm_g):
    b, n, d = x.shape
    nctx = ctx.shape[1]
    depth = w_ada.shape[0]
    assert nctx == TT and n % TT == 0 and n % GRID_W == 0 and b <= SUBLANES - 1
    assert w_router.shape[-1] == N_EXPERTS and w_in.shape[-1] == IN_WIDTH
    s = nctx + n
    nt = s // TT
    cap_lat = CAPACITY_FACTOR * n // N_EXPERTS
    cap_ctx = CAPACITY_FACTOR * nctx // N_EXPERTS
    assert cap_ctx <= DISPATCH_ROUND and cap_lat % BF16_ROWS == 0
    lat_rows = b * cap_lat
    ctx_rows = b * DISPATCH_ROUND

    cond = jnp.zeros((SUBLANES, d), F32).at[:b].set(c).at[b].set(c_ctx)
    mods = _ada(cond, w_ada, b_ada)
    cos, sin = _rope_tables(n, nctx)
    ones_q = _block_diag_ones(A_WIDTH)
    ones_k = _block_diag_ones(A_KV_WIDTH)
    tri = jnp.asarray(np.triu(np.ones((TT, TT))), BF16)

    stream = jnp.concatenate([ctx, x], axis=1)
    slot_rows = lat_rows + ctx_rows
    segments = ((0, 1, cap_ctx, lat_rows, DISPATCH_ROUND), (1, nt - 1, cap_lat, 0, cap_lat))
    bias_rows = n // GRID_W
    for i in range(depth):
        m = mods[i]
        modsel = jnp.stack([jnp.broadcast_to(m[b], (b, 6 * d)), m[:b]], axis=1)[:, :, None, :]
        qg = jnp.tile(q_norm_g[i], A_HEADS)[None, :]
        kg = jnp.tile(k_norm_g[i], A_KV_HEADS)[None, :]
        qa, ka, va, qb, kb, vb, u = _in_proj(stream, modsel, norm1_g[i][None, :], w_in[i].astype(BF16),
                                             cos, sin, qg, kg, ones_q, ones_k)
        a_mix = _attn_a(qa, ka, va, nctx)
        b_mix = _attn_b(qb, kb, vb, _na_bias_tables(na_rpb[i], bias_rows), nctx)
        wr = jnp.zeros((d, LANES), F32).at[:, :N_EXPERTS].set(w_router[i])
        wr_hi = wr.astype(BF16)
        wr_lo = (wr - wr_hi.astype(F32)).astype(BF16)
        x1, h2, aff_t, gate3 = _mix_out(a_mix, b_mix, u, stream, modsel, conv_w[i], conv_b[i][None, :],
                                        conv_ln_g[i][None, :], conv_ln_b[i][None, :], w_out[i].astype(BF16),
                                        norm2_g[i][None, :], wr_hi, wr_lo)
        pos, pos_t, start_t, cnt_t = _route(aff_t, tri, segments)
        starts = jnp.transpose(start_t, (0, 2, 1))[:, :nt, :].reshape(-1)
        cnts = jnp.transpose(cnt_t, (0, 2, 1))[:, :nt, :].reshape(-1)
        xe = _dispatch(starts, cnts, h2, pos, gate3, slot_rows)
        y = _ffn(xe, w_gate[i].astype(BF16), w_up[i].astype(BF16), w_down[i].astype(BF16), slot_rows)
        stream = _combine(starts, cnts, x1, pos_t, modsel, final_norm_g[None, :], y, slot_rows, i == depth - 1)
    return stream
```

```python
import functools

import numpy as np
import jax
import jax.numpy as jnp
from jax import lax
from jax.experimental import pallas as pl
from jax.experimental.pallas import tpu as pltpu

F32 = jnp.float32
BF16 = jnp.bfloat16
I32 = jnp.int32

GRID_W = 64
HEAD_DIM = 64
A_HEADS = 8
A_KV_HEADS = 2
A_GROUP = A_HEADS // A_KV_HEADS
B_HEADS = 4
C_CHANNELS = 256
CONV_WIDTH = 31
NA_WIN_ROWS = 8
NA_WIN_COLS = 16
N_EXPERTS = 16
CAPACITY_FACTOR = 2
ROPE_THETA = 10000.0
EPS = 1e-6
A_WIDTH = A_HEADS * HEAD_DIM
A_KV_WIDTH = A_KV_HEADS * HEAD_DIM
B_WIDTH = B_HEADS * HEAD_DIM
MIX_WIDTH = A_WIDTH + B_WIDTH + C_CHANNELS
IN_WIDTH = A_WIDTH + 2 * A_KV_WIDTH + 3 * B_WIDTH + 2 * C_CHANNELS

LANES = 128
SUBLANES = 8
BF16_ROWS = 16
VMEM_LIMIT = 56 * 1024 * 1024

TT = 256
NEG = -1e30
LOG2_E = float(np.log2(np.e))
CONV_HALO = 16
DISPATCH_ROUND = 64
COMBINE_WINDOW = 64
COMBINE_ROUND = COMBINE_WINDOW - BF16_ROWS


def _cparams(n_axes):
    return pltpu.CompilerParams(dimension_semantics=("arbitrary",) * n_axes, vmem_limit_bytes=VMEM_LIMIT)


def _split3(x):
    hi = x.astype(BF16)
    r1 = x - hi.astype(F32)
    mid = r1.astype(BF16)
    lo = (r1 - mid.astype(F32)).astype(BF16)
    return hi, mid, lo


def _dot(a, b):
    return jnp.dot(a, b, preferred_element_type=F32)


def _dot_nt(a, b):
    return lax.dot_general(a, b, (((1,), (1,)), ((), ())), preferred_element_type=F32)


def _dot3(a, b_hi, b_lo):
    a_hi = a.astype(BF16)
    a_lo = (a - a_hi.astype(F32)).astype(BF16)
    return _dot(a_hi, b_hi) + (_dot(a_lo, b_hi) + _dot(a_hi, b_lo))


def _ada_kernel(cond_ref, w_ref, b_ref, o_ref):
    c = cond_ref[...]
    s = c * jax.nn.sigmoid(c)
    w = w_ref[0]
    w_hi = w.astype(BF16)
    w_lo = (w - w_hi.astype(F32)).astype(BF16)
    o_ref[0] = _dot3(s, w_hi, w_lo) + b_ref[0]


def _ada(cond, w_ada, b_ada):
    depth, d, six_d = w_ada.shape
    bn = six_d // 4
    return pl.pallas_call(
        _ada_kernel,
        grid=(depth, six_d // bn),
        in_specs=[pl.BlockSpec((SUBLANES, d), lambda l, j: (0, 0)),
                  pl.BlockSpec((1, d, bn), lambda l, j: (l, 0, j)),
                  pl.BlockSpec((1, 1, bn), lambda l, j: (l, 0, j))],
        out_specs=pl.BlockSpec((1, SUBLANES, bn), lambda l, j: (l, 0, j)),
        out_shape=jax.ShapeDtypeStruct((depth, SUBLANES, six_d), F32),
        compiler_params=_cparams(2),
        name="ada",
    )(cond, w_ada, b_ada.reshape(depth, 1, six_d))


def _head_rms(x, ones_bd, g):
    sq = x * x
    hi = sq.astype(BF16)
    lo = (sq - hi.astype(F32)).astype(BF16)
    ssum = _dot(hi, ones_bd) + _dot(lo, ones_bd)
    return x * lax.rsqrt(ssum * (1.0 / HEAD_DIM) + EPS) * g


def _rope(x, cos, sin_signed):
    w = x.shape[-1]
    lane = lax.broadcasted_iota(I32, x.shape, 1)
    first_half = (lane % 32) < 16
    partner = jnp.where(first_half, pltpu.roll(x, w - 16, 1), pltpu.roll(x, 16, 1))
    return x * cos + partner * sin_signed


def _in_proj_kernel(x_ref, mod_ref, g1_ref, w_ref, cos_ref, sin_ref, qg_ref, kg_ref, onesq_ref, onesk_ref,
                    qa_ref, ka_ref, va_ref, qb_ref, kb_ref, vb_ref, u_ref, *, d):
    x = x_ref[...]
    y = x * lax.rsqrt(jnp.mean(x * x, axis=-1, keepdims=True) + EPS) * g1_ref[...]
    shift = mod_ref[:, 0:d]
    scale = mod_ref[:, d:2 * d]
    h = (y * (1.0 + scale) + shift).astype(BF16)
    p = _dot(h, w_ref[...])
    o = 0
    qa = p[:, o:o + A_WIDTH]; o += A_WIDTH
    ka = p[:, o:o + A_KV_WIDTH]; o += A_KV_WIDTH
    va = p[:, o:o + A_KV_WIDTH]; o += A_KV_WIDTH
    qb = p[:, o:o + B_WIDTH]; o += B_WIDTH
    kb = p[:, o:o + B_WIDTH]; o += B_WIDTH
    vb = p[:, o:o + B_WIDTH]; o += B_WIDTH
    cv = p[:, o:o + C_CHANNELS]; o += C_CHANNELS
    cg = p[:, o:o + C_CHANNELS]

    cos = cos_ref[...]
    sin = sin_ref[...]
    sm_scale = HEAD_DIM ** -0.5
    qa = _head_rms(qa, onesq_ref[...], qg_ref[...])
    qa = _rope(qa, jnp.concatenate([cos] * (A_WIDTH // LANES), axis=1),
               jnp.concatenate([sin] * (A_WIDTH // LANES), axis=1)) * (sm_scale * LOG2_E)
    ka = _head_rms(ka, onesk_ref[...], kg_ref[...])
    ka = _rope(ka, cos, sin)
    qa_ref[...] = qa.T.reshape(A_HEADS, HEAD_DIM, -1).astype(BF16)
    va_ref[...] = va.T.reshape(A_KV_HEADS, HEAD_DIM, -1).astype(BF16)
    for hh in range(A_KV_HEADS):
        ka_ref[hh] = ka[:, hh * HEAD_DIM:(hh + 1) * HEAD_DIM].astype(BF16)
    qb_ref[...] = (qb * sm_scale).astype(BF16)
    kb_ref[...] = kb.astype(BF16)
    vb_ref[...] = vb.astype(BF16)
    u_ref[...] = cv * jax.nn.sigmoid(cg)


def _in_proj(x, modsel, g1, w_in, cos, sin, qg, kg, ones_q, ones_k):
    b, s, d = x.shape
    nt = s // TT
    tile = lambda bi, t: (bi, t, 0)
    head_tile = lambda bi, t: (bi, 0, t, 0)
    head_tile_t = lambda bi, t: (bi, 0, 0, t)
    const2 = lambda bi, t: (0, 0)
    return pl.pallas_call(
        functools.partial(_in_proj_kernel, d=d),
        grid=(b, nt),
        in_specs=[pl.BlockSpec((None, TT, d), tile),
                  pl.BlockSpec((None, None, 1, 6 * d), lambda bi, t: (bi, jnp.minimum(t, 1), 0, 0)),
                  pl.BlockSpec((1, d), const2),
                  pl.BlockSpec((d, IN_WIDTH), const2),
                  pl.BlockSpec((TT, LANES), lambda bi, t: (t, 0)),
                  pl.BlockSpec((TT, LANES), lambda bi, t: (t, 0)),
                  pl.BlockSpec((1, A_WIDTH), const2),
                  pl.BlockSpec((1, A_KV_WIDTH), const2),
                  pl.BlockSpec((A_WIDTH, A_WIDTH), const2),
                  pl.BlockSpec((A_KV_WIDTH, A_KV_WIDTH), const2)],
        out_specs=[pl.BlockSpec((None, A_HEADS, HEAD_DIM, TT), head_tile_t),
                   pl.BlockSpec((None, A_KV_HEADS, TT, HEAD_DIM), head_tile),
                   pl.BlockSpec((None, A_KV_HEADS, HEAD_DIM, TT), head_tile_t),
                   pl.BlockSpec((None, TT, B_WIDTH), tile),
                   pl.BlockSpec((None, TT, B_WIDTH), tile),
                   pl.BlockSpec((None, TT, B_WIDTH), tile),
                   pl.BlockSpec((None, TT, C_CHANNELS), tile)],
        out_shape=[jax.ShapeDtypeStruct((b, A_HEADS, HEAD_DIM, s), BF16),
                   jax.ShapeDtypeStruct((b, A_KV_HEADS, s, HEAD_DIM), BF16),
                   jax.ShapeDtypeStruct((b, A_KV_HEADS, HEAD_DIM, s), BF16),
                   jax.ShapeDtypeStruct((b, s, B_WIDTH), BF16),
                   jax.ShapeDtypeStruct((b, s, B_WIDTH), BF16),
                   jax.ShapeDtypeStruct((b, s, B_WIDTH), BF16),
                   jax.ShapeDtypeStruct((b, s, C_CHANNELS), F32)],
        compiler_params=_cparams(2),
        name="in_proj",
    )(x, modsel, g1, w_in, cos, sin, qg, kg, ones_q, ones_k)


def _attn_a_kernel(q_ref, k_ref, v_ref, o_ref, *, ctx, tk, n_chunks):
    t = pl.program_id(2)
    n_heads, dh, tq = q_ref.shape
    g = n_heads // 2
    qs = [jnp.concatenate([q_ref[2 * i], q_ref[2 * i + 1]], axis=1) for i in range(g)]

    def update(state, s, v):
        m, l, acc = state
        m_new = jnp.maximum(m, jnp.max(s, axis=0, keepdims=True))
        alpha = jnp.exp2(m - m_new)
        p = jnp.exp2(s - m_new)
        l = alpha * l + jnp.sum(p, axis=0, keepdims=True)
        return m_new, l, alpha * acc + _dot(v, p.astype(BF16))

    def step(s0, k, v, k_next, states):
        out = []
        s = s0
        for i in range(g):
            s_next = _dot(k, qs[i + 1]) if i + 1 < g else _dot(k_next, qs[0])
            out.append(update(states[i], s, v))
            s = s_next
        return s, tuple(out)

    def chunk(c):
        return pl.ds(pl.multiple_of(ctx + c * tk, int(np.gcd(ctx, tk))), tk)

    init = tuple((jnp.full((1, 2 * tq), NEG, F32), jnp.zeros((1, 2 * tq), F32), jnp.zeros((dh, 2 * tq), F32))
                 for _ in range(g))
    k_ctx = k_ref[0:ctx, :]
    v_ctx = v_ref[:, 0:ctx]
    s0 = _dot(k_ctx, qs[0])

    def finish(states):
        o_t = jnp.concatenate([part for (_, l, acc) in states
                               for part in ((acc / l)[:, 0:tq], (acc / l)[:, tq:2 * tq])], axis=0)
        o_ref[...] = o_t.T.astype(o_ref.dtype)

    @pl.when(t == 0)
    def _():
        _, states = step(s0, k_ctx, v_ctx, k_ctx, init)
        finish(states)

    @pl.when(t > 0)
    def _():
        carry = step(s0, k_ctx, v_ctx, k_ref[chunk(0), :], init)

        def body(c, carry):
            s_pair0, states = carry
            nxt = jnp.minimum(c + 1, n_chunks - 1)
            return step(s_pair0, k_ref[chunk(c), :], v_ref[:, chunk(c)], k_ref[chunk(nxt), :], states)

        finish(lax.fori_loop(0, n_chunks, body, carry, unroll=True)[1])


def _attn_a(qa_t, ka, va_t, ctx):
    b, _, dh, s = qa_t.shape
    nt = s // TT
    tk = 1024 if (s - ctx) % 1024 == 0 else TT
    return pl.pallas_call(
        functools.partial(_attn_a_kernel, ctx=ctx, tk=tk, n_chunks=(s - ctx) // tk),
        grid=(b, A_KV_HEADS, nt),
        in_specs=[pl.BlockSpec((None, A_GROUP, dh, TT), lambda bi, j, t: (bi, j, 0, t)),
                  pl.BlockSpec((None, None, s, dh), lambda bi, j, t: (bi, j, 0, 0)),
                  pl.BlockSpec((None, None, dh, s), lambda bi, j, t: (bi, j, 0, 0))],
        out_specs=pl.BlockSpec((None, TT, A_GROUP * dh), lambda bi, j, t: (bi, t, j)),
        out_shape=jax.ShapeDtypeStruct((b, s, A_WIDTH), BF16),
        compiler_params=_cparams(3),
        name="attn_a",
    )(qa_t, ka, va_t)


def _attn_b_kernel(q_ref, k_ref, v_ref, bias_ref, o_ref, *, ctx, rows, wr):
    t = pl.program_id(1)
    tt, w = q_ref.shape
    rows_per_tile = tt // GRID_W
    win = wr * GRID_W
    q = q_ref[...]
    kc = k_ref[0:ctx, :]
    vc = v_ref[0:ctx, :]
    lane_head = lax.broadcasted_iota(I32, (tt, w), 1) // HEAD_DIM

    def softmax_parts(parts):
        m = parts[0].max(axis=1, keepdims=True)
        for s in parts[1:]:
            m = jnp.maximum(m, s.max(axis=1, keepdims=True))
        es = [jnp.exp(s - m) for s in parts]
        tot = es[0].sum(axis=1, keepdims=True)
        for e in es[1:]:
            tot = tot + e.sum(axis=1, keepdims=True)
        inv = 1.0 / tot
        return [(e * inv).astype(BF16) for e in es]

    @pl.when(t == 0)
    def _():
        out = jnp.zeros((tt, w), F32)
        for hh in range(B_HEADS):
            qh = jnp.where(lane_head == hh, q, jnp.zeros_like(q))
            (p,) = softmax_parts([_dot_nt(qh, kc)])
            out = out + jnp.where(lane_head == hh, _dot(p, vc), 0.0)
        o_ref[...] = out.astype(o_ref.dtype)

    @pl.when(t > 0)
    def _():
        r0 = (t - 1) * rows_per_tile
        k_rows, v_rows, variant = [], [], []
        for i in range(rows_per_tile):
            r = r0 + i
            rs = jnp.clip(r - wr // 2, 0, rows - wr)
            start = pl.multiple_of(ctx + rs * GRID_W, GRID_W)
            k_rows.append(k_ref[pl.ds(start, win), :])
            v_rows.append(v_ref[pl.ds(start, win), :])
            variant.append(r - rs)
        scores = []
        for hh in range(B_HEADS):
            qh = jnp.where(lane_head == hh, q, jnp.zeros_like(q))
            s_nb = jnp.concatenate([_dot_nt(qh[i * GRID_W:(i + 1) * GRID_W], k_rows[i]) + bias_ref[variant[i], hh]
                                    for i in range(rows_per_tile)], axis=0)
            scores.append([_dot_nt(qh, kc), s_nb])
        probs = [softmax_parts(sc) for sc in scores]
        out = jnp.zeros((tt, w), F32)
        for hh in range(B_HEADS):
            p_cx, p_nb = probs[hh]
            o_nb = jnp.concatenate([_dot(p_nb[i * GRID_W:(i + 1) * GRID_W], v_rows[i])
                                    for i in range(rows_per_tile)], axis=0)
            out = out + jnp.where(lane_head == hh, _dot(p_cx, vc) + o_nb, 0.0)
        o_ref[...] = out.astype(o_ref.dtype)


def _attn_b(qb, kb, vb, bias, ctx):
    b, s, w = qb.shape
    nt = s // TT
    rows = (s - ctx) // GRID_W
    wr = min(NA_WIN_ROWS, rows)
    tile = lambda bi, t: (bi, t, 0)
    whole = lambda bi, t: (bi, 0, 0)
    return pl.pallas_call(
        functools.partial(_attn_b_kernel, ctx=ctx, rows=rows, wr=wr),
        grid=(b, nt),
        in_specs=[pl.BlockSpec((None, TT, w), tile),
                  pl.BlockSpec((None, s, w), whole),
                  pl.BlockSpec((None, s, w), whole),
                  pl.BlockSpec(bias.shape, lambda bi, t: (0, 0, 0, 0))],
        out_specs=pl.BlockSpec((None, TT, w), tile),
        out_shape=jax.ShapeDtypeStruct((b, s, w), BF16),
        compiler_params=_cparams(2),
        name="attn_b",
    )(qb, kb, vb, bias)


def _na_bias_tables(rpb, rows):
    wr = min(NA_WIN_ROWS, rows)
    wc = NA_WIN_COLS
    q = np.arange(GRID_W)
    cs = np.clip(q - wc // 2, 0, GRID_W - wc)
    kc = np.arange(GRID_W)
    valid = (kc[None, :] >= cs[:, None]) & (kc[None, :] < cs[:, None] + wc)
    dc = np.clip(kc[None, :] - q[:, None] + (NA_WIN_COLS - 1), 0, 2 * NA_WIN_COLS - 2)
    v = np.arange(wr)
    wrow = np.arange(wr)
    dr = wrow[None, :] - v[:, None] + (NA_WIN_ROWS - 1)
    pick_r = jnp.asarray(dr[None, :, :] == np.arange(2 * NA_WIN_ROWS - 1)[:, None, None], F32)
    pick_c = jnp.asarray((dc[None, :, :] == np.arange(2 * NA_WIN_COLS - 1)[:, None, None])
                         & valid[None, :, :], F32)
    tab = jnp.einsum("hrc,rvw,cqk->vhqwk", rpb.astype(F32), pick_r, pick_c, precision=lax.Precision.HIGHEST)
    tab = tab + jnp.asarray(np.where(valid, 0.0, NEG), F32)[None, None, :, None, :]
    return tab.reshape(wr, B_HEADS, GRID_W, wr * GRID_W)


def _mix_out_kernel(a_ref, b_ref, up_ref, uc_ref, un_ref, x_ref, mod_ref, cw_ref, cb_ref, lg_ref, lb_ref,
                    wo_ref, g2_ref, wrh_ref, wrl_ref,
                    x1_ref, h2_ref, aff_ref, gate_ref, *, d, nt):
    t = pl.program_id(1)
    tt = uc_ref.shape[0]
    prev_ok = t >= 2
    next_ok = jnp.logical_and(t >= 1, t < nt - 1)
    prev = jnp.where(prev_ok, up_ref[tt - CONV_HALO:tt, :], 0.0)
    nxt = jnp.where(next_ok, un_ref[0:CONV_HALO, :], 0.0)
    win = jnp.concatenate([prev, uc_ref[...], nxt], axis=0)
    off = CONV_HALO - CONV_WIDTH // 2
    acc = jnp.zeros((tt, C_CHANNELS), F32) + cb_ref[...]
    for k in range(CONV_WIDTH):
        acc = acc + win[off + k:off + k + tt, :] * cw_ref[k:k + 1, :]
    mu = jnp.mean(acc, axis=-1, keepdims=True)
    var = jnp.mean(jnp.square(acc - mu), axis=-1, keepdims=True)
    yc = (acc - mu) * lax.rsqrt(var + EPS) * lg_ref[...] + lb_ref[...]
    c = (yc * jax.nn.sigmoid(yc)).astype(BF16)

    mix = (_dot(a_ref[...], wo_ref[0:A_WIDTH, :])
           + _dot(b_ref[...], wo_ref[A_WIDTH:A_WIDTH + B_WIDTH, :])
           + _dot(c, wo_ref[A_WIDTH + B_WIDTH:MIX_WIDTH, :]))
    g1 = mod_ref[:, 2 * d:3 * d]
    x1 = x_ref[...] + g1 * mix
    x1_ref[...] = x1
    y = x1 * lax.rsqrt(jnp.mean(x1 * x1, axis=-1, keepdims=True) + EPS) * g2_ref[...]
    h2 = y * (1.0 + mod_ref[:, 4 * d:5 * d]) + mod_ref[:, 3 * d:4 * d]
    h2_ref[...] = h2.astype(BF16)
    logits = _dot3(h2, wrh_ref[...], wrl_ref[...])
    lane = lax.broadcasted_iota(I32, logits.shape, 1)
    logits = jnp.where(lane < N_EXPERTS, logits, NEG)
    e = jnp.exp(logits - jnp.max(logits, axis=-1, keepdims=True))
    aff = e / jnp.sum(e, axis=-1, keepdims=True)
    aff_ref[...] = aff.T[0:N_EXPERTS, :]
    hi, mid, lo = _split3(aff)
    gate = jnp.where(lane < N_EXPERTS, hi.astype(F32),
                     jnp.where(lane < 2 * N_EXPERTS, pltpu.roll(mid.astype(F32), N_EXPERTS, 1),
                               jnp.where(lane < 3 * N_EXPERTS, pltpu.roll(lo.astype(F32), 2 * N_EXPERTS, 1), 0.0)))
    gate_ref[...] = gate.astype(BF16)


def _mix_out(a, bmix, u, x, modsel, cw, cb, lg, lb, wo, g2, wr_hi, wr_lo):
    b, s, d = x.shape
    nt = s // TT
    tile = lambda bi, t: (bi, t, 0)
    const2 = lambda bi, t: (0, 0)
    return pl.pallas_call(
        functools.partial(_mix_out_kernel, d=d, nt=nt),
        grid=(b, nt),
        in_specs=[pl.BlockSpec((None, TT, A_WIDTH), tile),
                  pl.BlockSpec((None, TT, B_WIDTH), tile),
                  pl.BlockSpec((None, TT, C_CHANNELS), lambda bi, t: (bi, jnp.maximum(t - 1, 0), 0)),
                  pl.BlockSpec((None, TT, C_CHANNELS), tile),
                  pl.BlockSpec((None, TT, C_CHANNELS), lambda bi, t: (bi, jnp.minimum(t + 1, nt - 1), 0)),
                  pl.BlockSpec((None, TT, d), tile),
                  pl.BlockSpec((None, None, 1, 6 * d), lambda bi, t: (bi, jnp.minimum(t, 1), 0, 0)),
                  pl.BlockSpec((CONV_WIDTH, C_CHANNELS), const2),
                  pl.BlockSpec((1, C_CHANNELS), const2),
                  pl.BlockSpec((1, C_CHANNELS), const2),
                  pl.BlockSpec((1, C_CHANNELS), const2),
                  pl.BlockSpec((MIX_WIDTH, d), const2),
                  pl.BlockSpec((1, d), const2),
                  pl.BlockSpec((d, LANES), const2),
                  pl.BlockSpec((d, LANES), const2)],
        out_specs=[pl.BlockSpec((None, TT, d), tile),
                   pl.BlockSpec((None, TT, d), tile),
                   pl.BlockSpec((None, N_EXPERTS, TT), lambda bi, t: (bi, 0, t)),
                   pl.BlockSpec((None, TT, LANES), tile)],
        out_shape=[jax.ShapeDtypeStruct((b, s, d), F32),
                   jax.ShapeDtypeStruct((b, s, d), BF16),
                   jax.ShapeDtypeStruct((b, N_EXPERTS, s), F32),
                   jax.ShapeDtypeStruct((b, s, LANES), BF16)],
        compiler_params=_cparams(2),
        name="mix_out",
    )(a, bmix, u, u, u, x, modsel, cw, cb, lg, lb, wo, g2, wr_hi, wr_lo)


def _route_kernel(aff_ref, tri_ref, pos_ref, post_ref, start_ref, cnt_ref, *, segments):
    bi = pl.program_id(0)
    tri = tri_ref[...]
    ne = N_EXPERTS
    lane_t = lax.broadcasted_iota(I32, (ne, LANES), 1)
    starts = jnp.zeros((ne, LANES), F32)
    cnts = jnp.zeros((ne, LANES), F32)

    def block_cumsum(mask_f32, carry):
        inc = _dot(mask_f32.astype(BF16), tri) + carry
        return inc - mask_f32, inc[:, TT - 1:TT]

    for (tile0, ntiles, cap, row0, rows_per_sample) in segments:
        lo = tile0 * TT
        n = ntiles * TT
        bits = pltpu.bitcast(aff_ref[:, lo:lo + n], I32)

        def search(i, cur):
            cand = cur | lax.shift_left(jnp.int32(1), 30 - i)
            c = jnp.sum(jnp.where(bits >= cand, 1.0, 0.0), axis=1, keepdims=True)
            return jnp.where(c >= cap, cand, cur)

        thr = lax.fori_loop(0, 31, search, jnp.zeros((ne, 1), I32))
        n_gt = jnp.sum(jnp.where(bits > thr, 1.0, 0.0), axis=1, keepdims=True)
        need = cap - n_gt
        base = (row0 + bi * rows_per_sample).astype(F32)
        tie_carry = jnp.zeros((ne, 1), F32)
        pos_carry = jnp.zeros((ne, 1), F32)
        for k in range(ntiles):
            bk = bits[:, k * TT:(k + 1) * TT]
            eq = jnp.where(bk == thr, 1.0, 0.0)
            tie_rank, tie_carry = block_cumsum(eq, tie_carry)
            sel = jnp.where(bk > thr, 1.0, jnp.where(tie_rank < need, eq, 0.0))
            rank, new_carry = block_cumsum(sel, pos_carry)
            pos = jnp.where(sel > 0.0, rank + base, -1.0)
            pos_ref[:, lo + k * TT:lo + (k + 1) * TT] = pos.astype(I32)
            padded = jnp.concatenate([pos, jnp.full((LANES - ne, TT), -1.0, F32)], axis=0)
            post_ref[lo + k * TT:lo + (k + 1) * TT, :] = padded.T.astype(I32)
            starts = jnp.where(lane_t == tile0 + k, pos_carry + base, starts)
            cnts = jnp.where(lane_t == tile0 + k, new_carry - pos_carry, cnts)
            pos_carry = new_carry
    start_ref[...] = starts.astype(I32)
    cnt_ref[...] = cnts.astype(I32)


def _route(aff_t, tri, segments):
    b, ne, s = aff_t.shape
    return pl.pallas_call(
        functools.partial(_route_kernel, segments=segments),
        grid=(b,),
        in_specs=[pl.BlockSpec((None, ne, s), lambda bi: (bi, 0, 0)),
                  pl.BlockSpec((TT, TT), lambda bi: (0, 0))],
        out_specs=[pl.BlockSpec((None, ne, s), lambda bi: (bi, 0, 0)),
                   pl.BlockSpec((None, s, LANES), lambda bi: (bi, 0, 0)),
                   pl.BlockSpec((None, ne, LANES), lambda bi: (bi, 0, 0)),
                   pl.BlockSpec((None, ne, LANES), lambda bi: (bi, 0, 0))],
        out_shape=[jax.ShapeDtypeStruct((b, ne, s), I32),
                   jax.ShapeDtypeStruct((b, s, LANES), I32),
                   jax.ShapeDtypeStruct((b, ne, LANES), I32),
                   jax.ShapeDtypeStruct((b, ne, LANES), I32)],
        compiler_params=_cparams(1),
        name="route",
    )(aff_t, tri)


def _step_to_tile(step, n_lat_steps, lat_tiles):
    is_lat = step < n_lat_steps
    bi = jnp.where(is_lat, step // lat_tiles, step - n_lat_steps)
    t = jnp.where(is_lat, 1 + step % lat_tiles, 0)
    return bi, t


def _dispatch_kernel(start_sm, cnt_sm, h_ref, pos_ref, gate_ref, xe_ref, xbuf, xextra, carry, sems,
                     *, n_lat_steps, lat_tiles, nt, d, slot_rows):
    step = pl.program_id(0)
    last_step = pl.num_programs(0) - 1
    slot = step % 2
    bi, t = _step_to_tile(step, n_lat_steps, lat_tiles)
    ne = N_EXPERTS
    rnd = DISPATCH_ROUND
    sub = SUBLANES
    width = xextra.shape[1]

    @pl.when(step == 0)
    def _():
        carry[...] = jnp.zeros_like(carry)
        xextra[0:rnd, :] = jnp.zeros((rnd, width), F32)
        pads = [pltpu.make_async_copy(xextra.at[pl.ds(0, rnd)], xe_ref.at[e, pl.ds(slot_rows, rnd)],
                                      sems.at[2, e]) for e in range(ne)]
        for cp in pads:
            cp.start()
        for cp in pads:
            cp.wait()

    sbase = (bi * nt + t) * ne
    starts = [start_sm[sbase + e] for e in range(ne)]
    cnts = [cnt_sm[sbase + e] for e in range(ne)]
    ends = [starts[e] + cnts[e] for e in range(ne)]
    a0 = [(starts[e] // sub) * sub for e in range(ne)]
    last_grp = [(ends[e] // sub) * sub for e in range(ne)]
    n_rounds = jnp.int32(0)
    for e in range(ne):
        n_rounds = jnp.maximum(n_rounds, jnp.where(cnts[e] > 0, (ends[e] - a0[e] + rnd - 1) // rnd, 0))
    h = h_ref[...]
    gates = gate_ref[...]
    pos = pos_ref[...]
    row = lax.broadcasted_iota(I32, (rnd, TT), 0)
    grp_row = lax.broadcasted_iota(I32, (sub, width), 0)

    def copy(e, r, buf, sem_row):
        dst = pl.ds(pl.multiple_of(a0[e] + r * rnd, sub), rnd)
        return pltpu.make_async_copy(buf.at[pl.ds(e * rnd, rnd)], xe_ref.at[e, dst], sems.at[sem_row, e])

    def active(e, r):
        return jnp.logical_and(cnts[e] > 0, a0[e] + r * rnd < ends[e])

    def fill(r, buf):
        sel = jnp.concatenate(
            [jnp.where(pos[e:e + 1, :] - (a0[e] + r * rnd) == row, 1.0, 0.0).astype(BF16)
             for e in range(ne)], axis=0)
        buf[:, 0:d] = _dot(sel, h)
        buf[:, d:] = _dot(sel, gates)
        if r == 0:
            for e in range(ne):
                buf[e * rnd:e * rnd + sub, :] += carry[e]
        for e in range(ne):
            partial = jnp.logical_and(ends[e] > last_grp[e], (last_grp[e] - a0[e]) // rnd == r)

            @pl.when(jnp.logical_and(cnts[e] > 0, partial))
            def _():
                off = pl.multiple_of(e * rnd + (last_grp[e] - a0[e]) - r * rnd, sub)
                carry[e] = jnp.where(grp_row < ends[e] - last_grp[e], buf[pl.ds(off, sub), :], 0.0)

    def later_round(r):
        fill(r, xextra)
        for e in range(ne):
            pl.when(active(e, r))(lambda e=e: copy(e, r, xextra, 2).start())
        for e in range(ne):
            pl.when(active(e, r))(lambda e=e: copy(e, r, xextra, 2).wait())

    pl.when(n_rounds > 0)(lambda: fill(0, xbuf.at[slot]))

    @pl.when(step > 0)
    def _():
        pbi, pt = _step_to_tile(jnp.maximum(step - 1, 0), n_lat_steps, lat_tiles)
        pbase = (pbi * nt + pt) * ne
        for e in range(ne):
            @pl.when(cnt_sm[pbase + e] > 0)
            def _():
                pltpu.make_async_copy(xbuf.at[1 - slot, pl.ds(e * rnd, rnd)], xe_ref.at[e, pl.ds(0, rnd)],
                                      sems.at[1 - slot, e]).wait()

    for e in range(ne):
        pl.when(cnts[e] > 0)(lambda e=e: copy(e, 0, xbuf.at[slot], slot).start())
    for r in range(1, (TT + sub - 1 + rnd - 1) // rnd):
        pl.when(r < n_rounds)(functools.partial(later_round, r))
    for e in range(ne):
        @pl.when(jnp.logical_and(cnts[e] > 0, ends[e] == last_grp[e]))
        def _():
            carry[e] = jnp.zeros((sub, width), F32)

    @pl.when(step == last_step)
    def _():
        for e in range(ne):
            pl.when(cnts[e] > 0)(lambda e=e: copy(e, 0, xbuf.at[slot], slot).wait())


def _dispatch(starts, cnts, h2, pos, gate3, slot_rows):
    b, s, d = h2.shape
    nt = s // TT
    lat_tiles = nt - 1
    n_lat_steps = b * lat_tiles
    steps = n_lat_steps + b
    decode = functools.partial(_step_to_tile, n_lat_steps=n_lat_steps, lat_tiles=lat_tiles)

    def tile(step, *_):
        bi, t = decode(step)
        return bi, t, 0

    def pos_tile(step, *_):
        bi, t = decode(step)
        return bi, 0, t

    rows = N_EXPERTS * DISPATCH_ROUND
    width = d + LANES
    return pl.pallas_call(
        functools.partial(_dispatch_kernel, n_lat_steps=n_lat_steps, lat_tiles=lat_tiles, nt=nt, d=d,
                          slot_rows=slot_rows),
        grid_spec=pltpu.PrefetchScalarGridSpec(
            num_scalar_prefetch=2,
            grid=(steps,),
            in_specs=[pl.BlockSpec((None, TT, d), tile),
                      pl.BlockSpec((None, N_EXPERTS, TT), pos_tile),
                      pl.BlockSpec((None, TT, LANES), tile)],
            out_specs=pl.BlockSpec(memory_space=pl.ANY),
            scratch_shapes=[pltpu.VMEM((2, rows, width), F32),
                            pltpu.VMEM((rows, width), F32),
                            pltpu.VMEM((N_EXPERTS, SUBLANES, width), F32),
                            pltpu.SemaphoreType.DMA((3, N_EXPERTS))]),
        out_shape=jax.ShapeDtypeStruct((N_EXPERTS, slot_rows + DISPATCH_ROUND, width), F32),
        compiler_params=_cparams(1),
        name="dispatch",
    )(starts, cnts, h2, pos, gate3)


def _ffn_kernel(x_ref, wg_hbm, wu_hbm, wd_hbm, y_ref, wg_bf, wu_bf, wd_bf, sg, su, sd, sems, *, d, layer, n_tiles):
    e = pl.program_id(0)
    j = pl.program_id(1)
    cur = e % 2
    rows_g = sg.shape[0]
    rows_d = sd.shape[0]

    def slab_copies(expert, slab):
        return (pltpu.make_async_copy(wg_hbm.at[layer, expert, pl.ds(slab * rows_g, rows_g)], sg, sems.at[0]),
                pltpu.make_async_copy(wu_hbm.at[layer, expert, pl.ds(slab * rows_g, rows_g)], su, sems.at[1]),
                pltpu.make_async_copy(wd_hbm.at[layer, expert, pl.ds(slab * rows_d, rows_d)], sd, sems.at[2]))

    def cast_slab(half, slab):
        og = pl.multiple_of(slab * rows_g, BF16_ROWS)
        od = pl.multiple_of(slab * rows_d, BF16_ROWS)
        wg_bf[half, pl.ds(og, rows_g), :] = sg[...].astype(BF16)
        wu_bf[half, pl.ds(og, rows_g), :] = su[...].astype(BF16)
        wd_bf[half, pl.ds(od, rows_d), :] = sd[...].astype(BF16)

    @pl.when(jnp.logical_and(e == 0, j == 0))
    def _():
        for slab in range(n_tiles):
            cps = slab_copies(0, slab)
            for cp in cps:
                cp.start()
            for cp in cps:
                cp.wait()
            cast_slab(0, slab)

    has_next = e + 1 < pl.num_programs(0)

    @pl.when(has_next)
    def _():
        for cp in slab_copies(e + 1, j):
            cp.start()

    x = x_ref[:, 0:d].astype(BF16)
    gt = _dot(x, wg_bf[cur])
    up = _dot(x, wu_bf[cur])
    hid = (gt * jax.nn.sigmoid(gt) * up).astype(BF16)
    y = _dot(hid, wd_bf[cur])
    g3 = x_ref[:, d:]
    lane = lax.broadcasted_iota(I32, g3.shape, 1)
    mine = jnp.logical_and(lane % N_EXPERTS == e, lane < 3 * N_EXPERTS)
    gate = jnp.sum(jnp.where(mine, g3, 0.0), axis=1, keepdims=True)
    y_ref[...] = (y * gate).astype(y_ref.dtype)

    @pl.when(has_next)
    def _():
        for cp in slab_copies(e + 1, j):
            cp.wait()
        cast_slab(1 - cur, j)


def _ffn_tile(rows):
    best = BF16_ROWS
    for m in range(BF16_ROWS, 641, BF16_ROWS):
        if rows % m == 0:
            best = m
    return best


def _ffn(xe, w_gate, w_up, w_down, layer, slot_rows):
    ne, _, width = xe.shape
    d, hdim = w_gate.shape[-2:]
    mt = _ffn_tile(slot_rows)
    n_tiles = slot_rows // mt
    assert d % (n_tiles * BF16_ROWS) == 0 and hdim % (n_tiles * BF16_ROWS) == 0
    return pl.pallas_call(
        functools.partial(_ffn_kernel, d=d, layer=layer, n_tiles=n_tiles),
        grid=(ne, n_tiles),
        in_specs=[pl.BlockSpec((None, mt, width), lambda e, j: (e, j, 0)),
                  pl.BlockSpec(memory_space=pl.ANY),
                  pl.BlockSpec(memory_space=pl.ANY),
                  pl.BlockSpec(memory_space=pl.ANY)],
        out_specs=pl.BlockSpec((None, mt, d), lambda e, j: (e, j, 0)),
        out_shape=jax.ShapeDtypeStruct((ne, slot_rows, d), BF16),
        scratch_shapes=[pltpu.VMEM((2, d, hdim), BF16),
                        pltpu.VMEM((2, d, hdim), BF16),
                        pltpu.VMEM((2, hdim, d), BF16),
                        pltpu.VMEM((d // n_tiles, hdim), F32),
                        pltpu.VMEM((d // n_tiles, hdim), F32),
                        pltpu.VMEM((hdim // n_tiles, d), F32),
                        pltpu.SemaphoreType.DMA((3,))],
        compiler_params=_cparams(2),
        name="ffn",
    )(xe, w_gate, w_up, w_down)


def _combine_kernel(start_sm, cnt_sm, x_ref, post_ref, mod_ref, fg_ref, y_ref, o_ref, ybuf, yextra, acc_ref, sems,
                    *, d, nt, slot_rows, final):
    t0 = 1 if final else 0
    bi = pl.program_id(0)
    tt = pl.program_id(1)
    tiles = pl.num_programs(1)
    lin = bi * tiles + tt
    slot = lin % 2
    ne = N_EXPERTS
    win = COMBINE_WINDOW
    sbase = (bi * nt + tt + t0) * ne
    starts = [start_sm[sbase + e] for e in range(ne)]
    cmax = cnt_sm[sbase]
    for e in range(1, ne):
        cmax = jnp.maximum(cmax, cnt_sm[sbase + e])
    post = post_ref[...]
    lane = lax.broadcasted_iota(I32, (TT, win), 1)

    def window_rows(first_slots, r):
        return [jnp.minimum((first_slots[e] // BF16_ROWS) * BF16_ROWS + r * COMBINE_ROUND, slot_rows - win)
                for e in range(ne)]

    def fetch(rows, dst, sem_row):
        return [pltpu.make_async_copy(y_ref.at[e, pl.ds(pl.multiple_of(rows[e], BF16_ROWS), win)],
                                      dst.at[pl.ds(e * win, win)], sems.at[sem_row, e]) for e in range(ne)]

    def add_round(r, rows, src):
        sel = []
        for e in range(ne):
            rel = post[:, e:e + 1] - starts[e]
            in_round = jnp.logical_and(rel >= r * COMBINE_ROUND, rel < (r + 1) * COMBINE_ROUND)
            hit = jnp.logical_and(in_round, post[:, e:e + 1] - rows[e] == lane)
            sel.append(jnp.where(hit, 1.0, 0.0).astype(BF16))
        acc_ref[...] += _dot(jnp.concatenate(sel, axis=1), src[...])

    rows0 = window_rows(starts, 0)

    @pl.when(lin == 0)
    def _():
        for cp in fetch(rows0, ybuf.at[slot], slot):
            cp.start()

    @pl.when(lin + 1 < pl.num_programs(0) * tiles)
    def _():
        nxt = lin + 1
        nbase = ((nxt // tiles) * nt + nxt % tiles + t0) * ne
        for cp in fetch(window_rows([start_sm[nbase + e] for e in range(ne)], 0), ybuf.at[1 - slot], 1 - slot):
            cp.start()

    acc_ref[...] = jnp.zeros_like(acc_ref)
    for cp in fetch(rows0, ybuf.at[slot], slot):
        cp.wait()
    add_round(0, rows0, ybuf.at[slot])

    def later_round(r):
        rows = window_rows(starts, r)
        cps = fetch(rows, yextra, 2)
        for cp in cps:
            cp.start()
        for cp in cps:
            cp.wait()
        add_round(r, rows, yextra)

    n_rounds = (cmax + COMBINE_ROUND - 1) // COMBINE_ROUND
    for r in range(1, (TT + COMBINE_ROUND - 1) // COMBINE_ROUND):
        pl.when(r < n_rounds)(functools.partial(later_round, r))
    x2 = x_ref[...] + mod_ref[:, 5 * d:6 * d] * acc_ref[...]
    if final:
        x2 = x2 * lax.rsqrt(jnp.mean(x2 * x2, axis=-1, keepdims=True) + EPS) * fg_ref[...]
    o_ref[...] = x2


def _combine(starts, cnts, x1, pos_t, modsel, final_g, y, slot_rows, final):
    b, s, d = x1.shape
    nt = s // TT
    t0 = 1 if final else 0
    tile = lambda bi, t, *_: (bi, t + t0, 0)
    out_rows = s - t0 * TT if final else s
    out_tile = (lambda bi, t, *_: (bi, t, 0)) if final else tile
    return pl.pallas_call(
        functools.partial(_combine_kernel, d=d, nt=nt, slot_rows=slot_rows, final=final),
        grid_spec=pltpu.PrefetchScalarGridSpec(
            num_scalar_prefetch=2,
            grid=(b, nt - t0),
            in_specs=[pl.BlockSpec((None, TT, d), tile),
                      pl.BlockSpec((None, TT, LANES), tile),
                      pl.BlockSpec((None, None, 1, 6 * d), lambda bi, t, *_: (bi, jnp.minimum(t + t0, 1), 0, 0)),
                      pl.BlockSpec((1, d), lambda bi, t, *_: (0, 0)),
                      pl.BlockSpec(memory_space=pl.ANY)],
            out_specs=pl.BlockSpec((None, TT, d), out_tile),
            scratch_shapes=[pltpu.VMEM((2, N_EXPERTS * COMBINE_WINDOW, d), BF16),
                            pltpu.VMEM((N_EXPERTS * COMBINE_WINDOW, d), BF16),
                            pltpu.VMEM((TT, d), F32),
                            pltpu.SemaphoreType.DMA((3, N_EXPERTS))]),
        out_shape=jax.ShapeDtypeStruct((b, out_rows, d), F32),
        compiler_params=_cparams(2),
        name="combine",
    )(starts, cnts, x1, pos_t, modsel, final_g, y)


def _rope_tables(n, ctx):
    tkn = np.arange(n)
    half = HEAD_DIM // 4
    inv = ROPE_THETA ** (-np.arange(half, dtype=np.float64) / half)
    ang_row = (tkn // GRID_W)[:, None] * inv[None, :]
    ang_col = (tkn % GRID_W)[:, None] * inv[None, :]
    ang = np.concatenate([ang_row, ang_row, ang_col, ang_col], axis=1)
    sign = np.tile(np.concatenate([-np.ones(half), np.ones(half)]), 2)
    cos = np.concatenate([np.ones((ctx, HEAD_DIM)), np.cos(ang)], axis=0)
    sin = np.concatenate([np.zeros((ctx, HEAD_DIM)), np.sin(ang) * sign[None, :]], axis=0)
    reps = LANES // HEAD_DIM
    return (jnp.asarray(np.tile(cos, (1, reps)), F32), jnp.asarray(np.tile(sin, (1, reps)), F32))


def _block_diag_ones(width):
    idx = np.arange(width) // HEAD_DIM
    return jnp.asarray(idx[:, None] == idx[None, :], BF16)


def kernel(x, c, ctx, c_ctx, w_ada, b_ada, norm1_g, norm2_g, w_in, q_norm_g, k_norm_g, na_rpb, conv_w, conv_b,
           conv_ln_g, conv_ln_b, w_out, w_router, w_gate, w_up, w_down, final_norm_g):
    b, n, d = x.shape
    nctx = ctx.shape[1]
    depth = w_ada.shape[0]
    assert nctx == TT and n % TT == 0 and n % GRID_W == 0 and b <= SUBLANES - 1
    assert w_router.shape[-1] == N_EXPERTS and w_in.shape[-1] == IN_WIDTH
    s = nctx + n
    nt = s // TT
    cap_lat = CAPACITY_FACTOR * n // N_EXPERTS
    cap_ctx = CAPACITY_FACTOR * nctx // N_EXPERTS
    assert cap_ctx <= DISPATCH_ROUND and cap_lat % BF16_ROWS == 0
    lat_rows = b * cap_lat
    ctx_rows = b * DISPATCH_ROUND

    cond = jnp.zeros((SUBLANES, d), F32).at[:b].set(c).at[b].set(c_ctx)
    mods = _ada(cond, w_ada, b_ada)
    cos, sin = _rope_tables(n, nctx)
    ones_q = _block_diag_ones(A_WIDTH)
    ones_k = _block_diag_ones(A_KV_WIDTH)
    tri = jnp.asarray(np.triu(np.ones((TT, TT))), BF16)

    stream = jnp.concatenate([ctx, x], axis=1)
    slot_rows = lat_rows + ctx_rows
    segments = ((0, 1, cap_ctx, lat_rows, DISPATCH_ROUND), (1, nt - 1, cap_lat, 0, cap_lat))
    bias_rows = n // GRID_W
    for i in range(depth):
        m = mods[i]
        modsel = jnp.stack([jnp.broadcast_to(m[b], (b, 6 * d)), m[:b]], axis=1)[:, :, None, :]
        qg = jnp.tile(q_norm_g[i], A_HEADS)[None, :]
        kg = jnp.tile(k_norm_g[i], A_KV_HEADS)[None, :]
        qa, ka, va, qb, kb, vb, u = _in_proj(stream, modsel, norm1_g[i][None, :], w_in[i].astype(BF16),
                                             cos, sin, qg, kg, ones_q, ones_k)
        a_mix = _attn_a(qa, ka, va, nctx)
        b_mix = _attn_b(qb, kb, vb, _na_bias_tables(na_rpb[i], bias_rows), nctx)
        wr = jnp.zeros((d, LANES), F32).at[:, :N_EXPERTS].set(w_router[i])
        wr_hi = wr.astype(BF16)
        wr_lo = (wr - wr_hi.astype(F32)).astype(BF16)
        x1, h2, aff_t, gate3 = _mix_out(a_mix, b_mix, u, stream, modsel, conv_w[i], conv_b[i][None, :],
                                        conv_ln_g[i][None, :], conv_ln_b[i][None, :], w_out[i].astype(BF16),
                                        norm2_g[i][None, :], wr_hi, wr_lo)
        pos, pos_t, start_t, cnt_t = _route(aff_t, tri, segments)
        starts = jnp.transpose(start_t, (0, 2, 1))[:, :nt, :].reshape(-1)
        cnts = jnp.transpose(cnt_t, (0, 2, 1))[:, :nt, :].reshape(-1)
        xe = _dispatch(starts, cnts, h2, pos, gate3, slot_rows)
        y = _ffn(xe, w_gate, w_up, w_down, i, slot_rows)
        stream = _combine(starts, cnts, x1, pos_t, modsel, final_norm_g[None, :], y, slot_rows, i == depth - 1)
    return stream
```

```python
import functools

import numpy as np
import jax
import jax.numpy as jnp
from jax import lax
from jax.experimental import pallas as pl
from jax.experimental.pallas import tpu as pltpu

F32 = jnp.float32
BF16 = jnp.bfloat16
I32 = jnp.int32

GRID_W = 64
HEAD_DIM = 64
A_HEADS = 8
A_KV_HEADS = 2
A_GROUP = A_HEADS // A_KV_HEADS
B_HEADS = 4
C_CHANNELS = 256
CONV_WIDTH = 31
NA_WIN_ROWS = 8
NA_WIN_COLS = 16
N_EXPERTS = 16
CAPACITY_FACTOR = 2
ROPE_THETA = 10000.0
EPS = 1e-6
A_WIDTH = A_HEADS * HEAD_DIM
A_KV_WIDTH = A_KV_HEADS * HEAD_DIM
B_WIDTH = B_HEADS * HEAD_DIM
MIX_WIDTH = A_WIDTH + B_WIDTH + C_CHANNELS
IN_WIDTH = A_WIDTH + 2 * A_KV_WIDTH + 3 * B_WIDTH + 2 * C_CHANNELS

LANES = 128
SUBLANES = 8
BF16_ROWS = 16
VMEM_LIMIT = 56 * 1024 * 1024

TT = 256
NEG = -1e30
LOG2_E = float(np.log2(np.e))
CONV_HALO = 16
DISPATCH_ROUND = 64
A_HEADS_PER_MATMUL = 2
A_KEY_CHUNK = 1024
COMBINE_WINDOW = 64
COMBINE_ROUND = COMBINE_WINDOW - BF16_ROWS


def _cparams(n_axes):
    return pltpu.CompilerParams(dimension_semantics=("arbitrary",) * n_axes, vmem_limit_bytes=VMEM_LIMIT)


def _split3(x):
    hi = x.astype(BF16)
    r1 = x - hi.astype(F32)
    mid = r1.astype(BF16)
    lo = (r1 - mid.astype(F32)).astype(BF16)
    return hi, mid, lo


def _dot(a, b):
    return jnp.dot(a, b, preferred_element_type=F32)


def _dot_nt(a, b):
    return lax.dot_general(a, b, (((1,), (1,)), ((), ())), preferred_element_type=F32)


def _dot3(a, b_hi, b_lo):
    a_hi = a.astype(BF16)
    a_lo = (a - a_hi.astype(F32)).astype(BF16)
    return _dot(a_hi, b_hi) + (_dot(a_lo, b_hi) + _dot(a_hi, b_lo))


def _ada_kernel(cond_ref, w_ref, b_ref, o_ref):
    c = cond_ref[...]
    s = c * jax.nn.sigmoid(c)
    w = w_ref[0]
    w_hi = w.astype(BF16)
    w_lo = (w - w_hi.astype(F32)).astype(BF16)
    o_ref[0] = _dot3(s, w_hi, w_lo) + b_ref[0]


def _ada(cond, w_ada, b_ada):
    depth, d, six_d = w_ada.shape
    bn = six_d // 4
    return pl.pallas_call(
        _ada_kernel,
        grid=(depth, six_d // bn),
        in_specs=[pl.BlockSpec((SUBLANES, d), lambda l, j: (0, 0)),
                  pl.BlockSpec((1, d, bn), lambda l, j: (l, 0, j)),
                  pl.BlockSpec((1, 1, bn), lambda l, j: (l, 0, j))],
        out_specs=pl.BlockSpec((1, SUBLANES, bn), lambda l, j: (l, 0, j)),
        out_shape=jax.ShapeDtypeStruct((depth, SUBLANES, six_d), F32),
        compiler_params=_cparams(2),
        name="ada",
    )(cond, w_ada, b_ada.reshape(depth, 1, six_d))


def _head_rms(x, ones_bd, g):
    sq = x * x
    hi = sq.astype(BF16)
    lo = (sq - hi.astype(F32)).astype(BF16)
    ssum = _dot(hi, ones_bd) + _dot(lo, ones_bd)
    return x * lax.rsqrt(ssum * (1.0 / HEAD_DIM) + EPS) * g


def _rope(x, cos, sin_signed):
    w = x.shape[-1]
    lane = lax.broadcasted_iota(I32, x.shape, 1)
    first_half = (lane % 32) < 16
    partner = jnp.where(first_half, pltpu.roll(x, w - 16, 1), pltpu.roll(x, 16, 1))
    return x * cos + partner * sin_signed


def _in_proj_kernel(x_ref, mod_ref, g1_ref, w_ref, cos_ref, sin_ref, qg_ref, kg_ref, onesq_ref, onesk_ref,
                    qa_ref, ka_ref, va_ref, qb_ref, kb_ref, vb_ref, u_ref, *, d):
    x = x_ref[...]
    y = x * lax.rsqrt(jnp.mean(x * x, axis=-1, keepdims=True) + EPS) * g1_ref[...]
    shift = mod_ref[:, 0:d]
    scale = mod_ref[:, d:2 * d]
    h = (y * (1.0 + scale) + shift).astype(BF16)
    p = _dot(h, w_ref[...])
    o = 0
    qa = p[:, o:o + A_WIDTH]; o += A_WIDTH
    ka = p[:, o:o + A_KV_WIDTH]; o += A_KV_WIDTH
    va = p[:, o:o + A_KV_WIDTH]; o += A_KV_WIDTH
    qb = p[:, o:o + B_WIDTH]; o += B_WIDTH
    kb = p[:, o:o + B_WIDTH]; o += B_WIDTH
    vb = p[:, o:o + B_WIDTH]; o += B_WIDTH
    cv = p[:, o:o + C_CHANNELS]; o += C_CHANNELS
    cg = p[:, o:o + C_CHANNELS]

    cos = cos_ref[...]
    sin = sin_ref[...]
    sm_scale = HEAD_DIM ** -0.5
    qa = _head_rms(qa, onesq_ref[...], qg_ref[...])
    qa = _rope(qa, jnp.concatenate([cos] * (A_WIDTH // LANES), axis=1),
               jnp.concatenate([sin] * (A_WIDTH // LANES), axis=1)) * (sm_scale * LOG2_E)
    ka = _head_rms(ka, onesk_ref[...], kg_ref[...])
    ka = _rope(ka, cos, sin)
    qa_ref[...] = qa.T.reshape(A_HEADS, HEAD_DIM, -1).astype(BF16)
    va_ref[...] = va.T.reshape(A_KV_HEADS, HEAD_DIM, -1).astype(BF16)
    for hh in range(A_KV_HEADS):
        ka_ref[hh] = ka[:, hh * HEAD_DIM:(hh + 1) * HEAD_DIM].astype(BF16)
    qb_ref[...] = (qb * sm_scale).astype(BF16)
    kb_ref[...] = kb.astype(BF16)
    vb_ref[...] = vb.astype(BF16)
    u_ref[...] = cv * jax.nn.sigmoid(cg)


def _in_proj(x, modsel, g1, w_in, cos, sin, qg, kg, ones_q, ones_k):
    b, s, d = x.shape
    nt = s // TT
    tile = lambda bi, t: (bi, t, 0)
    head_tile = lambda bi, t: (bi, 0, t, 0)
    head_tile_t = lambda bi, t: (bi, 0, 0, t)
    const2 = lambda bi, t: (0, 0)
    return pl.pallas_call(
        functools.partial(_in_proj_kernel, d=d),
        grid=(b, nt),
        in_specs=[pl.BlockSpec((None, TT, d), tile),
                  pl.BlockSpec((None, None, 1, 6 * d), lambda bi, t: (bi, jnp.minimum(t, 1), 0, 0)),
                  pl.BlockSpec((1, d), const2),
                  pl.BlockSpec((d, IN_WIDTH), const2),
                  pl.BlockSpec((TT, LANES), lambda bi, t: (t, 0)),
                  pl.BlockSpec((TT, LANES), lambda bi, t: (t, 0)),
                  pl.BlockSpec((1, A_WIDTH), const2),
                  pl.BlockSpec((1, A_KV_WIDTH), const2),
                  pl.BlockSpec((A_WIDTH, A_WIDTH), const2),
                  pl.BlockSpec((A_KV_WIDTH, A_KV_WIDTH), const2)],
        out_specs=[pl.BlockSpec((None, A_HEADS, HEAD_DIM, TT), head_tile_t),
                   pl.BlockSpec((None, A_KV_HEADS, TT, HEAD_DIM), head_tile),
                   pl.BlockSpec((None, A_KV_HEADS, HEAD_DIM, TT), head_tile_t),
                   pl.BlockSpec((None, TT, B_WIDTH), tile),
                   pl.BlockSpec((None, TT, B_WIDTH), tile),
                   pl.BlockSpec((None, TT, B_WIDTH), tile),
                   pl.BlockSpec((None, TT, C_CHANNELS), tile)],
        out_shape=[jax.ShapeDtypeStruct((b, A_HEADS, HEAD_DIM, s), BF16),
                   jax.ShapeDtypeStruct((b, A_KV_HEADS, s, HEAD_DIM), BF16),
                   jax.ShapeDtypeStruct((b, A_KV_HEADS, HEAD_DIM, s), BF16),
                   jax.ShapeDtypeStruct((b, s, B_WIDTH), BF16),
                   jax.ShapeDtypeStruct((b, s, B_WIDTH), BF16),
                   jax.ShapeDtypeStruct((b, s, B_WIDTH), BF16),
                   jax.ShapeDtypeStruct((b, s, C_CHANNELS), F32)],
        compiler_params=_cparams(2),
        name="in_proj",
    )(x, modsel, g1, w_in, cos, sin, qg, kg, ones_q, ones_k)


def _attn_a_kernel(q_ref, k_ref, v_ref, o_ref, *, ctx, tk, n_chunks):
    t = pl.program_id(2)
    n_heads, dh, tq = q_ref.shape
    hpm = A_HEADS_PER_MATMUL
    g = n_heads // hpm
    qs = [jnp.concatenate([q_ref[hpm * i + h] for h in range(hpm)], axis=1) for i in range(g)]

    def update(state, s, v):
        m, l, acc = state
        m_new = jnp.maximum(m, jnp.max(s, axis=0, keepdims=True))
        alpha = jnp.exp2(m - m_new)
        p = jnp.exp2(s - m_new)
        l = alpha * l + jnp.sum(p, axis=0, keepdims=True)
        return m_new, l, alpha * acc + _dot(v, p.astype(BF16))

    def step(s0, k, v, k_next, states):
        out = []
        s = s0
        for i in range(g):
            s_next = _dot(k, qs[i + 1]) if i + 1 < g else _dot(k_next, qs[0])
            out.append(update(states[i], s, v))
            s = s_next
        return s, tuple(out)

    def chunk(c):
        return pl.ds(pl.multiple_of(ctx + c * tk, int(np.gcd(ctx, tk))), tk)

    init = tuple((jnp.full((1, hpm * tq), NEG, F32), jnp.zeros((1, hpm * tq), F32),
                  jnp.zeros((dh, hpm * tq), F32)) for _ in range(g))
    k_ctx = k_ref[0:ctx, :]
    v_ctx = v_ref[:, 0:ctx]
    s0 = _dot(k_ctx, qs[0])

    def finish(states):
        o_t = jnp.concatenate([(acc / l)[:, h * tq:(h + 1) * tq] for (_, l, acc) in states for h in range(hpm)],
                              axis=0)
        o_ref[...] = o_t.T.astype(o_ref.dtype)

    @pl.when(t == 0)
    def _():
        _, states = step(s0, k_ctx, v_ctx, k_ctx, init)
        finish(states)

    @pl.when(t > 0)
    def _():
        carry = step(s0, k_ctx, v_ctx, k_ref[chunk(0), :], init)

        def body(c, carry):
            s_pair0, states = carry
            nxt = jnp.minimum(c + 1, n_chunks - 1)
            return step(s_pair0, k_ref[chunk(c), :], v_ref[:, chunk(c)], k_ref[chunk(nxt), :], states)

        finish(lax.fori_loop(0, n_chunks, body, carry, unroll=True)[1])


def _attn_a(qa_t, ka, va_t, ctx):
    b, _, dh, s = qa_t.shape
    nt = s // TT
    tk = A_KEY_CHUNK if (s - ctx) % A_KEY_CHUNK == 0 else TT
    return pl.pallas_call(
        functools.partial(_attn_a_kernel, ctx=ctx, tk=tk, n_chunks=(s - ctx) // tk),
        grid=(b, A_KV_HEADS, nt),
        in_specs=[pl.BlockSpec((None, A_GROUP, dh, TT), lambda bi, j, t: (bi, j, 0, t)),
                  pl.BlockSpec((None, None, s, dh), lambda bi, j, t: (bi, j, 0, 0)),
                  pl.BlockSpec((None, None, dh, s), lambda bi, j, t: (bi, j, 0, 0))],
        out_specs=pl.BlockSpec((None, TT, A_GROUP * dh), lambda bi, j, t: (bi, t, j)),
        out_shape=jax.ShapeDtypeStruct((b, s, A_WIDTH), BF16),
        compiler_params=_cparams(3),
        name="attn_a",
    )(qa_t, ka, va_t)


def _attn_b_kernel(q_ref, k_ref, v_ref, bias_ref, o_ref, *, ctx, rows, wr):
    t = pl.program_id(1)
    tt, w = q_ref.shape
    rows_per_tile = tt // GRID_W
    win = wr * GRID_W
    q = q_ref[...]
    kc = k_ref[0:ctx, :]
    vc = v_ref[0:ctx, :]
    lane_head = lax.broadcasted_iota(I32, (tt, w), 1) // HEAD_DIM

    def softmax_parts(parts):
        m = parts[0].max(axis=1, keepdims=True)
        for s in parts[1:]:
            m = jnp.maximum(m, s.max(axis=1, keepdims=True))
        es = [jnp.exp(s - m) for s in parts]
        tot = es[0].sum(axis=1, keepdims=True)
        for e in es[1:]:
            tot = tot + e.sum(axis=1, keepdims=True)
        inv = 1.0 / tot
        return [(e * inv).astype(BF16) for e in es]

    @pl.when(t == 0)
    def _():
        out = jnp.zeros((tt, w), F32)
        for hh in range(B_HEADS):
            qh = jnp.where(lane_head == hh, q, jnp.zeros_like(q))
            (p,) = softmax_parts([_dot_nt(qh, kc)])
            out = out + jnp.where(lane_head == hh, _dot(p, vc), 0.0)
        o_ref[...] = out.astype(o_ref.dtype)

    @pl.when(t > 0)
    def _():
        r0 = (t - 1) * rows_per_tile
        k_rows, v_rows, variant = [], [], []
        for i in range(rows_per_tile):
            r = r0 + i
            rs = jnp.clip(r - wr // 2, 0, rows - wr)
            start = pl.multiple_of(ctx + rs * GRID_W, GRID_W)
            k_rows.append(k_ref[pl.ds(start, win), :])
            v_rows.append(v_ref[pl.ds(start, win), :])
            variant.append(r - rs)
        scores = []
        for hh in range(B_HEADS):
            qh = jnp.where(lane_head == hh, q, jnp.zeros_like(q))
            s_nb = jnp.concatenate([_dot_nt(qh[i * GRID_W:(i + 1) * GRID_W], k_rows[i]) + bias_ref[variant[i], hh]
                                    for i in range(rows_per_tile)], axis=0)
            scores.append([_dot_nt(qh, kc), s_nb])
        probs = [softmax_parts(sc) for sc in scores]
        out = jnp.zeros((tt, w), F32)
        for hh in range(B_HEADS):
            p_cx, p_nb = probs[hh]
            o_nb = jnp.concatenate([_dot(p_nb[i * GRID_W:(i + 1) * GRID_W], v_rows[i])
                                    for i in range(rows_per_tile)], axis=0)
            out = out + jnp.where(lane_head == hh, _dot(p_cx, vc) + o_nb, 0.0)
        o_ref[...] = out.astype(o_ref.dtype)


def _attn_b(qb, kb, vb, bias, ctx):
    b, s, w = qb.shape
    nt = s // TT
    rows = (s - ctx) // GRID_W
    wr = min(NA_WIN_ROWS, rows)
    tile = lambda bi, t: (bi, t, 0)
    whole = lambda bi, t: (bi, 0, 0)
    return pl.pallas_call(
        functools.partial(_attn_b_kernel, ctx=ctx, rows=rows, wr=wr),
        grid=(b, nt),
        in_specs=[pl.BlockSpec((None, TT, w), tile),
                  pl.BlockSpec((None, s, w), whole),
                  pl.BlockSpec((None, s, w), whole),
                  pl.BlockSpec(bias.shape, lambda bi, t: (0, 0, 0, 0))],
        out_specs=pl.BlockSpec((None, TT, w), tile),
        out_shape=jax.ShapeDtypeStruct((b, s, w), BF16),
        compiler_params=_cparams(2),
        name="attn_b",
    )(qb, kb, vb, bias)


def _na_bias_tables(rpb, rows):
    wr = min(NA_WIN_ROWS, rows)
    wc = NA_WIN_COLS
    q = np.arange(GRID_W)
    cs = np.clip(q - wc // 2, 0, GRID_W - wc)
    kc = np.arange(GRID_W)
    valid = (kc[None, :] >= cs[:, None]) & (kc[None, :] < cs[:, None] + wc)
    dc = np.clip(kc[None, :] - q[:, None] + (NA_WIN_COLS - 1), 0, 2 * NA_WIN_COLS - 2)
    v = np.arange(wr)
    wrow = np.arange(wr)
    dr = wrow[None, :] - v[:, None] + (NA_WIN_ROWS - 1)
    pick_r = jnp.asarray(dr[None, :, :] == np.arange(2 * NA_WIN_ROWS - 1)[:, None, None], F32)
    pick_c = jnp.asarray((dc[None, :, :] == np.arange(2 * NA_WIN_COLS - 1)[:, None, None])
                         & valid[None, :, :], F32)
    tab = jnp.einsum("hrc,rvw,cqk->vhqwk", rpb.astype(F32), pick_r, pick_c, precision=lax.Precision.HIGHEST)
    tab = tab + jnp.asarray(np.where(valid, 0.0, NEG), F32)[None, None, :, None, :]
    return tab.reshape(wr, B_HEADS, GRID_W, wr * GRID_W)


def _mix_out_kernel(a_ref, b_ref, up_ref, uc_ref, un_ref, x_ref, mod_ref, cw_ref, cb_ref, lg_ref, lb_ref,
                    wo_ref, g2_ref, wrh_ref, wrl_ref,
                    x1_ref, h2_ref, aff_ref, gate_ref, *, d, nt):
    t = pl.program_id(1)
    tt = uc_ref.shape[0]
    prev_ok = t >= 2
    next_ok = jnp.logical_and(t >= 1, t < nt - 1)
    prev = jnp.where(prev_ok, up_ref[tt - CONV_HALO:tt, :], 0.0)
    nxt = jnp.where(next_ok, un_ref[0:CONV_HALO, :], 0.0)
    win = jnp.concatenate([prev, uc_ref[...], nxt], axis=0)
    off = CONV_HALO - CONV_WIDTH // 2
    acc = jnp.zeros((tt, C_CHANNELS), F32) + cb_ref[...]
    span = tt + ((off + CONV_WIDTH - 1) // SUBLANES) * SUBLANES
    for b in range(SUBLANES):
        shifted = win[b:b + span, :]
        for k in range(CONV_WIDTH):
            if (off + k) % SUBLANES == b:
                a = (off + k) - b
                acc = acc + shifted[a:a + tt, :] * cw_ref[k:k + 1, :]
    mu = jnp.mean(acc, axis=-1, keepdims=True)
    var = jnp.mean(jnp.square(acc - mu), axis=-1, keepdims=True)
    yc = (acc - mu) * lax.rsqrt(var + EPS) * lg_ref[...] + lb_ref[...]
    c = (yc * jax.nn.sigmoid(yc)).astype(BF16)

    mix = (_dot(a_ref[...], wo_ref[0:A_WIDTH, :])
           + _dot(b_ref[...], wo_ref[A_WIDTH:A_WIDTH + B_WIDTH, :])
           + _dot(c, wo_ref[A_WIDTH + B_WIDTH:MIX_WIDTH, :]))
    g1 = mod_ref[:, 2 * d:3 * d]
    x1 = x_ref[...] + g1 * mix
    x1_ref[...] = x1
    y = x1 * lax.rsqrt(jnp.mean(x1 * x1, axis=-1, keepdims=True) + EPS) * g2_ref[...]
    h2 = y * (1.0 + mod_ref[:, 4 * d:5 * d]) + mod_ref[:, 3 * d:4 * d]
    h2_ref[...] = h2.astype(BF16)
    logits = _dot3(h2, wrh_ref[...], wrl_ref[...])
    lane = lax.broadcasted_iota(I32, logits.shape, 1)
    logits = jnp.where(lane < N_EXPERTS, logits, NEG)
    e = jnp.exp(logits - jnp.max(logits, axis=-1, keepdims=True))
    aff = e / jnp.sum(e, axis=-1, keepdims=True)
    aff_ref[...] = aff.T[0:N_EXPERTS, :]
    hi, mid, lo = _split3(aff)
    gate = jnp.where(lane < N_EXPERTS, hi.astype(F32),
                     jnp.where(lane < 2 * N_EXPERTS, pltpu.roll(mid.astype(F32), N_EXPERTS, 1),
                               jnp.where(lane < 3 * N_EXPERTS, pltpu.roll(lo.astype(F32), 2 * N_EXPERTS, 1), 0.0)))
    gate_ref[...] = gate.astype(BF16)


def _mix_out(a, bmix, u, x, modsel, cw, cb, lg, lb, wo, g2, wr_hi, wr_lo):
    b, s, d = x.shape
    nt = s // TT
    tile = lambda bi, t: (bi, t, 0)
    const2 = lambda bi, t: (0, 0)
    return pl.pallas_call(
        functools.partial(_mix_out_kernel, d=d, nt=nt),
        grid=(b, nt),
        in_specs=[pl.BlockSpec((None, TT, A_WIDTH), tile),
                  pl.BlockSpec((None, TT, B_WIDTH), tile),
                  pl.BlockSpec((None, TT, C_CHANNELS), lambda bi, t: (bi, jnp.maximum(t - 1, 0), 0)),
                  pl.BlockSpec((None, TT, C_CHANNELS), tile),
                  pl.BlockSpec((None, TT, C_CHANNELS), lambda bi, t: (bi, jnp.minimum(t + 1, nt - 1), 0)),
                  pl.BlockSpec((None, TT, d), tile),
                  pl.BlockSpec((None, None, 1, 6 * d), lambda bi, t: (bi, jnp.minimum(t, 1), 0, 0)),
                  pl.BlockSpec((CONV_WIDTH, C_CHANNELS), const2),
                  pl.BlockSpec((1, C_CHANNELS), const2),
                  pl.BlockSpec((1, C_CHANNELS), const2),
                  pl.BlockSpec((1, C_CHANNELS), const2),
                  pl.BlockSpec((MIX_WIDTH, d), const2),
                  pl.BlockSpec((1, d), const2),
                  pl.BlockSpec((d, LANES), const2),
                  pl.BlockSpec((d, LANES), const2)],
        out_specs=[pl.BlockSpec((None, TT, d), tile),
                   pl.BlockSpec((None, TT, d), tile),
                   pl.BlockSpec((None, N_EXPERTS, TT), lambda bi, t: (bi, 0, t)),
                   pl.BlockSpec((None, TT, LANES), tile)],
        out_shape=[jax.ShapeDtypeStruct((b, s, d), F32),
                   jax.ShapeDtypeStruct((b, s, d), BF16),
                   jax.ShapeDtypeStruct((b, N_EXPERTS, s), F32),
                   jax.ShapeDtypeStruct((b, s, LANES), BF16)],
        compiler_params=_cparams(2),
        name="mix_out",
    )(a, bmix, u, u, u, x, modsel, cw, cb, lg, lb, wo, g2, wr_hi, wr_lo)


def _route_kernel(aff_ref, tri_ref, pos_ref, post_ref, start_ref, cnt_ref, *, segments):
    bi = pl.program_id(0)
    tri = tri_ref[...]
    ne = N_EXPERTS
    lane_t = lax.broadcasted_iota(I32, (ne, LANES), 1)
    starts = jnp.zeros((ne, LANES), F32)
    cnts = jnp.zeros((ne, LANES), F32)

    def block_cumsum(mask_f32, carry):
        inc = _dot(mask_f32.astype(BF16), tri) + carry
        return inc - mask_f32, inc[:, TT - 1:TT]

    for (tile0, ntiles, cap, row0, rows_per_sample) in segments:
        lo = tile0 * TT
        n = ntiles * TT
        bits = pltpu.bitcast(aff_ref[:, lo:lo + n], I32)

        def search(i, cur):
            cand = cur | lax.shift_left(jnp.int32(1), 30 - i)
            c = jnp.sum(jnp.where(bits >= cand, 1.0, 0.0), axis=1, keepdims=True)
            return jnp.where(c >= cap, cand, cur)

        thr = lax.fori_loop(0, 31, search, jnp.zeros((ne, 1), I32))
        n_gt = jnp.sum(jnp.where(bits > thr, 1.0, 0.0), axis=1, keepdims=True)
        need = cap - n_gt
        base = (row0 + bi * rows_per_sample).astype(F32)
        tie_carry = jnp.zeros((ne, 1), F32)
        pos_carry = jnp.zeros((ne, 1), F32)
        for k in range(ntiles):
            bk = bits[:, k * TT:(k + 1) * TT]
            eq = jnp.where(bk == thr, 1.0, 0.0)
            tie_rank, tie_carry = block_cumsum(eq, tie_carry)
            sel = jnp.where(bk > thr, 1.0, jnp.where(tie_rank < need, eq, 0.0))
            rank, new_carry = block_cumsum(sel, pos_carry)
            pos = jnp.where(sel > 0.0, rank + base, -1.0)
            pos_ref[:, lo + k * TT:lo + (k + 1) * TT] = pos.astype(I32)
            padded = jnp.concatenate([pos, jnp.full((LANES - ne, TT), -1.0, F32)], axis=0)
            post_ref[lo + k * TT:lo + (k + 1) * TT, :] = padded.T.astype(I32)
            starts = jnp.where(lane_t == tile0 + k, pos_carry + base, starts)
            cnts = jnp.where(lane_t == tile0 + k, new_carry - pos_carry, cnts)
            pos_carry = new_carry
    start_ref[...] = starts.astype(I32)
    cnt_ref[...] = cnts.astype(I32)


def _route(aff_t, tri, segments):
    b, ne, s = aff_t.shape
    return pl.pallas_call(
        functools.partial(_route_kernel, segments=segments),
        grid=(b,),
        in_specs=[pl.BlockSpec((None, ne, s), lambda bi: (bi, 0, 0)),
                  pl.BlockSpec((TT, TT), lambda bi: (0, 0))],
        out_specs=[pl.BlockSpec((None, ne, s), lambda bi: (bi, 0, 0)),
                   pl.BlockSpec((None, s, LANES), lambda bi: (bi, 0, 0)),
                   pl.BlockSpec((None, ne, LANES), lambda bi: (bi, 0, 0)),
                   pl.BlockSpec((None, ne, LANES), lambda bi: (bi, 0, 0))],
        out_shape=[jax.ShapeDtypeStruct((b, ne, s), I32),
                   jax.ShapeDtypeStruct((b, s, LANES), I32),
                   jax.ShapeDtypeStruct((b, ne, LANES), I32),
                   jax.ShapeDtypeStruct((b, ne, LANES), I32)],
        compiler_params=_cparams(1),
        name="route",
    )(aff_t, tri)


def _step_to_tile(step, n_lat_steps, lat_tiles):
    is_lat = step < n_lat_steps
    bi = jnp.where(is_lat, step // lat_tiles, step - n_lat_steps)
    t = jnp.where(is_lat, 1 + step % lat_tiles, 0)
    return bi, t


def _dispatch_kernel(start_sm, cnt_sm, h_ref, pos_ref, gate_ref, xe_ref, xbuf, xextra, carry, sems,
                     *, n_lat_steps, lat_tiles, nt, d, slot_rows):
    step = pl.program_id(0)
    last_step = pl.num_programs(0) - 1
    slot = step % 2
    bi, t = _step_to_tile(step, n_lat_steps, lat_tiles)
    ne = N_EXPERTS
    rnd = DISPATCH_ROUND
    sub = SUBLANES
    width = xextra.shape[1]

    @pl.when(step == 0)
    def _():
        carry[...] = jnp.zeros_like(carry)
        xextra[0:rnd, :] = jnp.zeros((rnd, width), F32)
        pads = [pltpu.make_async_copy(xextra.at[pl.ds(0, rnd)], xe_ref.at[e, pl.ds(slot_rows, rnd)],
                                      sems.at[2, e]) for e in range(ne)]
        for cp in pads:
            cp.start()
        for cp in pads:
            cp.wait()

    sbase = (bi * nt + t) * ne
    starts = [start_sm[sbase + e] for e in range(ne)]
    cnts = [cnt_sm[sbase + e] for e in range(ne)]
    ends = [starts[e] + cnts[e] for e in range(ne)]
    a0 = [(starts[e] // sub) * sub for e in range(ne)]
    last_grp = [(ends[e] // sub) * sub for e in range(ne)]
    n_rounds = jnp.int32(0)
    for e in range(ne):
        n_rounds = jnp.maximum(n_rounds, jnp.where(cnts[e] > 0, (ends[e] - a0[e] + rnd - 1) // rnd, 0))
    h = h_ref[...]
    gates = gate_ref[...]
    pos = pos_ref[...]
    row = lax.broadcasted_iota(I32, (rnd, TT), 0)
    grp_row = lax.broadcasted_iota(I32, (sub, width), 0)

    def copy(e, r, buf, sem_row):
        dst = pl.ds(pl.multiple_of(a0[e] + r * rnd, sub), rnd)
        return pltpu.make_async_copy(buf.at[pl.ds(e * rnd, rnd)], xe_ref.at[e, dst], sems.at[sem_row, e])

    def active(e, r):
        return jnp.logical_and(cnts[e] > 0, a0[e] + r * rnd < ends[e])

    def fill(r, buf):
        sel = jnp.concatenate(
            [jnp.where(pos[e:e + 1, :] - (a0[e] + r * rnd) == row, 1.0, 0.0).astype(BF16)
             for e in range(ne)], axis=0)
        buf[:, 0:d] = _dot(sel, h)
        buf[:, d:] = _dot(sel, gates)
        if r == 0:
            for e in range(ne):
                buf[e * rnd:e * rnd + sub, :] += carry[e]
        for e in range(ne):
            partial = jnp.logical_and(ends[e] > last_grp[e], (last_grp[e] - a0[e]) // rnd == r)

            @pl.when(jnp.logical_and(cnts[e] > 0, partial))
            def _():
                off = pl.multiple_of(e * rnd + (last_grp[e] - a0[e]) - r * rnd, sub)
                carry[e] = jnp.where(grp_row < ends[e] - last_grp[e], buf[pl.ds(off, sub), :], 0.0)

    def later_round(r):
        fill(r, xextra)
        for e in range(ne):
            pl.when(active(e, r))(lambda e=e: copy(e, r, xextra, 2).start())
        for e in range(ne):
            pl.when(active(e, r))(lambda e=e: copy(e, r, xextra, 2).wait())

    pl.when(n_rounds > 0)(lambda: fill(0, xbuf.at[slot]))

    @pl.when(step > 0)
    def _():
        pbi, pt = _step_to_tile(jnp.maximum(step - 1, 0), n_lat_steps, lat_tiles)
        pbase = (pbi * nt + pt) * ne
        for e in range(ne):
            @pl.when(cnt_sm[pbase + e] > 0)
            def _():
                pltpu.make_async_copy(xbuf.at[1 - slot, pl.ds(e * rnd, rnd)], xe_ref.at[e, pl.ds(0, rnd)],
                                      sems.at[1 - slot, e]).wait()

    for e in range(ne):
        pl.when(cnts[e] > 0)(lambda e=e: copy(e, 0, xbuf.at[slot], slot).start())
    for r in range(1, (TT + sub - 1 + rnd - 1) // rnd):
        pl.when(r < n_rounds)(functools.partial(later_round, r))
    for e in range(ne):
        @pl.when(jnp.logical_and(cnts[e] > 0, ends[e] == last_grp[e]))
        def _():
            carry[e] = jnp.zeros((sub, width), F32)

    @pl.when(step == last_step)
    def _():
        for e in range(ne):
            pl.when(cnts[e] > 0)(lambda e=e: copy(e, 0, xbuf.at[slot], slot).wait())


def _dispatch(starts, cnts, h2, pos, gate3, slot_rows):
    b, s, d = h2.shape
    nt = s // TT
    lat_tiles = nt - 1
    n_lat_steps = b * lat_tiles
    steps = n_lat_steps + b
    decode = functools.partial(_step_to_tile, n_lat_steps=n_lat_steps, lat_tiles=lat_tiles)

    def tile(step, *_):
        bi, t = decode(step)
        return bi, t, 0

    def pos_tile(step, *_):
        bi, t = decode(step)
        return bi, 0, t

    rows = N_EXPERTS * DISPATCH_ROUND
    width = d + LANES
    return pl.pallas_call(
        functools.partial(_dispatch_kernel, n_lat_steps=n_lat_steps, lat_tiles=lat_tiles, nt=nt, d=d,
                          slot_rows=slot_rows),
        grid_spec=pltpu.PrefetchScalarGridSpec(
            num_scalar_prefetch=2,
            grid=(steps,),
            in_specs=[pl.BlockSpec((None, TT, d), tile),
                      pl.BlockSpec((None, N_EXPERTS, TT), pos_tile),
                      pl.BlockSpec((None, TT, LANES), tile)],
            out_specs=pl.BlockSpec(memory_space=pl.ANY),
            scratch_shapes=[pltpu.VMEM((2, rows, width), F32),
                            pltpu.VMEM((rows, width), F32),
                            pltpu.VMEM((N_EXPERTS, SUBLANES, width), F32),
                            pltpu.SemaphoreType.DMA((3, N_EXPERTS))]),
        out_shape=jax.ShapeDtypeStruct((N_EXPERTS, slot_rows + DISPATCH_ROUND, width), F32),
        compiler_params=_cparams(1),
        name="dispatch",
    )(starts, cnts, h2, pos, gate3)


def _ffn_kernel(x_ref, wg_hbm, wu_hbm, wd_hbm, y_ref, wg_bf, wu_bf, wd_bf, sg, su, sd, sems, *, d, layer, n_tiles):
    e = pl.program_id(0)
    j = pl.program_id(1)
    n_exp = pl.num_programs(0)
    cur = e % 2
    par = (e * n_tiles + j) % 2
    rows_g = sg.shape[1]
    rows_d = sd.shape[1]

    def slab_copies(expert, slab, buf):
        return (pltpu.make_async_copy(wg_hbm.at[layer, expert, pl.ds(slab * rows_g, rows_g)], sg.at[buf],
                                      sems.at[0, buf]),
                pltpu.make_async_copy(wu_hbm.at[layer, expert, pl.ds(slab * rows_g, rows_g)], su.at[buf],
                                      sems.at[1, buf]),
                pltpu.make_async_copy(wd_hbm.at[layer, expert, pl.ds(slab * rows_d, rows_d)], sd.at[buf],
                                      sems.at[2, buf]))

    def cast_slab(half, slab, buf):
        og = pl.multiple_of(slab * rows_g, BF16_ROWS)
        od = pl.multiple_of(slab * rows_d, BF16_ROWS)
        wg_bf[half, pl.ds(og, rows_g), :] = sg[buf].astype(BF16)
        wu_bf[half, pl.ds(og, rows_g), :] = su[buf].astype(BF16)
        wd_bf[half, pl.ds(od, rows_d), :] = sd[buf].astype(BF16)

    @pl.when(jnp.logical_and(e == 0, j == 0))
    def _():
        for slab in range(n_tiles):
            cps = slab_copies(0, slab, 0)
            for cp in cps:
                cp.start()
            for cp in cps:
                cp.wait()
            cast_slab(0, slab, 0)
        for cp in slab_copies(0, n_tiles - 1, 1):
            cp.start()

    e_prev = jnp.where(j > 0, e, e - 1)
    j_prev = jnp.where(j > 0, j - 1, n_tiles - 1)
    src_prev = jnp.minimum(e_prev + 1, n_exp - 1)
    for cp in slab_copies(src_prev, j_prev, 1 - par):
        cp.wait()
    for cp in slab_copies(jnp.minimum(e + 1, n_exp - 1), j, par):
        cp.start()
    cast_slab((e_prev + 1) % 2, j_prev, 1 - par)

    x = x_ref[:, 0:d].astype(BF16)
    gt = _dot(x, wg_bf[cur])
    up = _dot(x, wu_bf[cur])
    hid = (gt * jax.nn.sigmoid(gt) * up).astype(BF16)
    y = _dot(hid, wd_bf[cur])
    g3 = x_ref[:, d:]
    lane = lax.broadcasted_iota(I32, g3.shape, 1)
    mine = jnp.logical_and(lane % N_EXPERTS == e, lane < 3 * N_EXPERTS)
    gate = jnp.sum(jnp.where(mine, g3, 0.0), axis=1, keepdims=True)
    y_ref[...] = (y * gate).astype(y_ref.dtype)

    @pl.when(jnp.logical_and(e == n_exp - 1, j == n_tiles - 1))
    def _():
        for cp in slab_copies(n_exp - 1, j, par):
            cp.wait()


def _ffn_tile(rows):
    best = BF16_ROWS
    for m in range(BF16_ROWS, 641, BF16_ROWS):
        if rows % m == 0:
            best = m
    return best


def _ffn(xe, w_gate, w_up, w_down, layer, slot_rows):
    ne, _, width = xe.shape
    d, hdim = w_gate.shape[-2:]
    mt = _ffn_tile(slot_rows)
    n_tiles = slot_rows // mt
    assert d % (n_tiles * BF16_ROWS) == 0 and hdim % (n_tiles * BF16_ROWS) == 0
    return pl.pallas_call(
        functools.partial(_ffn_kernel, d=d, layer=layer, n_tiles=n_tiles),
        grid=(ne, n_tiles),
        in_specs=[pl.BlockSpec((None, mt, width), lambda e, j: (e, j, 0)),
                  pl.BlockSpec(memory_space=pl.ANY),
                  pl.BlockSpec(memory_space=pl.ANY),
                  pl.BlockSpec(memory_space=pl.ANY)],
        out_specs=pl.BlockSpec((None, mt, d), lambda e, j: (e, j, 0)),
        out_shape=jax.ShapeDtypeStruct((ne, slot_rows, d), BF16),
        scratch_shapes=[pltpu.VMEM((2, d, hdim), BF16),
                        pltpu.VMEM((2, d, hdim), BF16),
                        pltpu.VMEM((2, hdim, d), BF16),
                        pltpu.VMEM((2, d // n_tiles, hdim), F32),
                        pltpu.VMEM((2, d // n_tiles, hdim), F32),
                        pltpu.VMEM((2, hdim // n_tiles, d), F32),
                        pltpu.SemaphoreType.DMA((3, 2))],
        compiler_params=_cparams(2),
        name="ffn",
    )(xe, w_gate, w_up, w_down)


def _combine_kernel(start_sm, cnt_sm, x_ref, post_ref, mod_ref, fg_ref, y_ref, o_ref, ybuf, yextra, acc_ref, sems,
                    *, d, nt, slot_rows, final):
    t0 = 1 if final else 0
    bi = pl.program_id(0)
    tt = pl.program_id(1)
    tiles = pl.num_programs(1)
    lin = bi * tiles + tt
    slot = lin % 2
    ne = N_EXPERTS
    win = COMBINE_WINDOW
    sbase = (bi * nt + tt + t0) * ne
    starts = [start_sm[sbase + e] for e in range(ne)]
    cmax = cnt_sm[sbase]
    for e in range(1, ne):
        cmax = jnp.maximum(cmax, cnt_sm[sbase + e])
    post = post_ref[...]
    lane = lax.broadcasted_iota(I32, (TT, win), 1)

    def window_rows(first_slots, r):
        return [jnp.minimum((first_slots[e] // BF16_ROWS) * BF16_ROWS + r * COMBINE_ROUND, slot_rows - win)
                for e in range(ne)]

    def fetch(rows, dst, sem_row):
        return [pltpu.make_async_copy(y_ref.at[e, pl.ds(pl.multiple_of(rows[e], BF16_ROWS), win)],
                                      dst.at[pl.ds(e * win, win)], sems.at[sem_row, e]) for e in range(ne)]

    def add_round(r, rows, src):
        sel = []
        for e in range(ne):
            rel = post[:, e:e + 1] - starts[e]
            in_round = jnp.logical_and(rel >= r * COMBINE_ROUND, rel < (r + 1) * COMBINE_ROUND)
            hit = jnp.logical_and(in_round, post[:, e:e + 1] - rows[e] == lane)
            sel.append(jnp.where(hit, 1.0, 0.0).astype(BF16))
        acc_ref[...] += _dot(jnp.concatenate(sel, axis=1), src[...])

    rows0 = window_rows(starts, 0)

    @pl.when(lin == 0)
    def _():
        for cp in fetch(rows0, ybuf.at[slot], slot):
            cp.start()

    @pl.when(lin + 1 < pl.num_programs(0) * tiles)
    def _():
        nxt = lin + 1
        nbase = ((nxt // tiles) * nt + nxt % tiles + t0) * ne
        for cp in fetch(window_rows([start_sm[nbase + e] for e in range(ne)], 0), ybuf.at[1 - slot], 1 - slot):
            cp.start()

    acc_ref[...] = jnp.zeros_like(acc_ref)
    for cp in fetch(rows0, ybuf.at[slot], slot):
        cp.wait()
    add_round(0, rows0, ybuf.at[slot])

    def later_round(r):
        rows = window_rows(starts, r)
        cps = fetch(rows, yextra, 2)
        for cp in cps:
            cp.start()
        for cp in cps:
            cp.wait()
        add_round(r, rows, yextra)

    n_rounds = (cmax + COMBINE_ROUND - 1) // COMBINE_ROUND
    for r in range(1, (TT + COMBINE_ROUND - 1) // COMBINE_ROUND):
        pl.when(r < n_rounds)(functools.partial(later_round, r))
    x2 = x_ref[...] + mod_ref[:, 5 * d:6 * d] * acc_ref[...]
    if final:
        x2 = x2 * lax.rsqrt(jnp.mean(x2 * x2, axis=-1, keepdims=True) + EPS) * fg_ref[...]
    o_ref[...] = x2


def _combine(starts, cnts, x1, pos_t, modsel, final_g, y, slot_rows, final):
    b, s, d = x1.shape
    nt = s // TT
    t0 = 1 if final else 0
    tile = lambda bi, t, *_: (bi, t + t0, 0)
    out_rows = s - t0 * TT if final else s
    out_tile = (lambda bi, t, *_: (bi, t, 0)) if final else tile
    return pl.pallas_call(
        functools.partial(_combine_kernel, d=d, nt=nt, slot_rows=slot_rows, final=final),
        grid_spec=pltpu.PrefetchScalarGridSpec(
            num_scalar_prefetch=2,
            grid=(b, nt - t0),
            in_specs=[pl.BlockSpec((None, TT, d), tile),
                      pl.BlockSpec((None, TT, LANES), tile),
                      pl.BlockSpec((None, None, 1, 6 * d), lambda bi, t, *_: (bi, jnp.minimum(t + t0, 1), 0, 0)),
                      pl.BlockSpec((1, d), lambda bi, t, *_: (0, 0)),
                      pl.BlockSpec(memory_space=pl.ANY)],
            out_specs=pl.BlockSpec((None, TT, d), out_tile),
            scratch_shapes=[pltpu.VMEM((2, N_EXPERTS * COMBINE_WINDOW, d), BF16),
                            pltpu.VMEM((N_EXPERTS * COMBINE_WINDOW, d), BF16),
                            pltpu.VMEM((TT, d), F32),
                            pltpu.SemaphoreType.DMA((3, N_EXPERTS))]),
        out_shape=jax.ShapeDtypeStruct((b, out_rows, d), F32),
        compiler_params=_cparams(2),
        name="combine",
    )(starts, cnts, x1, pos_t, modsel, final_g, y)


def _rope_tables(n, ctx):
    tkn = np.arange(n)
    half = HEAD_DIM // 4
    inv = ROPE_THETA ** (-np.arange(half, dtype=np.float64) / half)
    ang_row = (tkn // GRID_W)[:, None] * inv[None, :]
    ang_col = (tkn % GRID_W)[:, None] * inv[None, :]
    ang = np.concatenate([ang_row, ang_row, ang_col, ang_col], axis=1)
    sign = np.tile(np.concatenate([-np.ones(half), np.ones(half)]), 2)
    cos = np.concatenate([np.ones((ctx, HEAD_DIM)), np.cos(ang)], axis=0)
    sin = np.concatenate([np.zeros((ctx, HEAD_DIM)), np.sin(ang) * sign[None, :]], axis=0)
    reps = LANES // HEAD_DIM
    return (jnp.asarray(np.tile(cos, (1, reps)), F32), jnp.asarray(np.tile(sin, (1, reps)), F32))


def _block_diag_ones(width):
    idx = np.arange(width) // HEAD_DIM
    return jnp.asarray(idx[:, None] == idx[None, :], BF16)


def kernel(x, c, ctx, c_ctx, w_ada, b_ada, norm1_g, norm2_g, w_in, q_norm_g, k_norm_g, na_rpb, conv_w, conv_b,
           conv_ln_g, conv_ln_b, w_out, w_router, w_gate, w_up, w_down, final_norm_g):
    b, n, d = x.shape
    nctx = ctx.shape[1]
    depth = w_ada.shape[0]
    assert nctx == TT and n % TT == 0 and n % GRID_W == 0 and b <= SUBLANES - 1
    assert w_router.shape[-1] == N_EXPERTS and w_in.shape[-1] == IN_WIDTH
    s = nctx + n
    nt = s // TT
    cap_lat = CAPACITY_FACTOR * n // N_EXPERTS
    cap_ctx = CAPACITY_FACTOR * nctx // N_EXPERTS
    assert cap_ctx <= DISPATCH_ROUND and cap_lat % BF16_ROWS == 0
    lat_rows = b * cap_lat
    ctx_rows = b * DISPATCH_ROUND

    cond = jnp.zeros((SUBLANES, d), F32).at[:b].set(c).at[b].set(c_ctx)
    mods = _ada(cond, w_ada, b_ada)
    cos, sin = _rope_tables(n, nctx)
    ones_q = _block_diag_ones(A_WIDTH)
    ones_k = _block_diag_ones(A_KV_WIDTH)
    tri = jnp.asarray(np.triu(np.ones((TT, TT))), BF16)

    stream = jnp.concatenate([ctx, x], axis=1)
    slot_rows = lat_rows + ctx_rows
    segments = ((0, 1, cap_ctx, lat_rows, DISPATCH_ROUND), (1, nt - 1, cap_lat, 0, cap_lat))
    bias_rows = n // GRID_W
    for i in range(depth):
        m = mods[i]
        modsel = jnp.stack([jnp.broadcast_to(m[b], (b, 6 * d)), m[:b]], axis=1)[:, :, None, :]
        qg = jnp.tile(q_norm_g[i], A_HEADS)[None, :]
        kg = jnp.tile(k_norm_g[i], A_KV_HEADS)[None, :]
        qa, ka, va, qb, kb, vb, u = _in_proj(stream, modsel, norm1_g[i][None, :], w_in[i].astype(BF16),
                                             cos, sin, qg, kg, ones_q, ones_k)
        a_mix = _attn_a(qa, ka, va, nctx)
        b_mix = _attn_b(qb, kb, vb, _na_bias_tables(na_rpb[i], bias_rows), nctx)
        wr = jnp.zeros((d, LANES), F32).at[:, :N_EXPERTS].set(w_router[i])
        wr_hi = wr.astype(BF16)
        wr_lo = (wr - wr_hi.astype(F32)).astype(BF16)
        x1, h2, aff_t, gate3 = _mix_out(a_mix, b_mix, u, stream, modsel, conv_w[i], conv_b[i][None, :],
                                        conv_ln_g[i][None, :], conv_ln_b[i][None, :], w_out[i].astype(BF16),
                                        norm2_g[i][None, :], wr_hi, wr_lo)
        pos, pos_t, start_t, cnt_t = _route(aff_t, tri, segments)
        starts = jnp.transpose(start_t, (0, 2, 1))[:, :nt, :].reshape(-1)
        cnts = jnp.transpose(cnt_t, (0, 2, 1))[:, :nt, :].reshape(-1)
        xe = _dispatch(starts, cnts, h2, pos, gate3, slot_rows)
        y = _ffn(xe, w_gate, w_up, w_down, i, slot_rows)
        stream = _combine(starts, cnts, x1, pos_t, modsel, final_norm_g[None, :], y, slot_rows, i == depth - 1)
    return stream
```

```python
import functools

import numpy as np
import jax
import jax.numpy as jnp
from jax import lax
from jax.experimental import pallas as pl
from jax.experimental.pallas import tpu as pltpu

F32 = jnp.float32
BF16 = jnp.bfloat16
I32 = jnp.int32

GRID_W = 64
HEAD_DIM = 64
A_HEADS = 8
A_KV_HEADS = 2
A_GROUP = A_HEADS // A_KV_HEADS
B_HEADS = 4
C_CHANNELS = 256
CONV_WIDTH = 31
NA_WIN_ROWS = 8
NA_WIN_COLS = 16
N_EXPERTS = 16
CAPACITY_FACTOR = 2
ROPE_THETA = 10000.0
EPS = 1e-6
A_WIDTH = A_HEADS * HEAD_DIM
A_KV_WIDTH = A_KV_HEADS * HEAD_DIM
B_WIDTH = B_HEADS * HEAD_DIM
MIX_WIDTH = A_WIDTH + B_WIDTH + C_CHANNELS
IN_WIDTH = A_WIDTH + 2 * A_KV_WIDTH + 3 * B_WIDTH + 2 * C_CHANNELS

LANES = 128
SUBLANES = 8
BF16_ROWS = 16
VMEM_LIMIT = 56 * 1024 * 1024

TT = 256
NEG = -1e30
LOG2_E = float(np.log2(np.e))
CONV_HALO = 16
DISPATCH_ROUND = 64
A_HEADS_PER_MATMUL = 2
A_KEY_CHUNK = 1024
COMBINE_WINDOW = 64
COMBINE_ROUND = COMBINE_WINDOW - BF16_ROWS


def _cparams(n_axes):
    return pltpu.CompilerParams(dimension_semantics=("arbitrary",) * n_axes, vmem_limit_bytes=VMEM_LIMIT)


def _split3(x):
    hi = x.astype(BF16)
    r1 = x - hi.astype(F32)
    mid = r1.astype(BF16)
    lo = (r1 - mid.astype(F32)).astype(BF16)
    return hi, mid, lo


def _dot(a, b):
    return jnp.dot(a, b, preferred_element_type=F32)


def _dot_nt(a, b):
    return lax.dot_general(a, b, (((1,), (1,)), ((), ())), preferred_element_type=F32)


def _dot3(a, b_hi, b_lo):
    a_hi = a.astype(BF16)
    a_lo = (a - a_hi.astype(F32)).astype(BF16)
    return _dot(a_hi, b_hi) + (_dot(a_lo, b_hi) + _dot(a_hi, b_lo))


def _ada_kernel(cond_ref, w_ref, b_ref, o_ref):
    c = cond_ref[...]
    s = c * jax.nn.sigmoid(c)
    w = w_ref[0]
    w_hi = w.astype(BF16)
    w_lo = (w - w_hi.astype(F32)).astype(BF16)
    o_ref[0] = _dot3(s, w_hi, w_lo) + b_ref[0]


def _ada(cond, w_ada, b_ada):
    depth, d, six_d = w_ada.shape
    bn = six_d // 4
    return pl.pallas_call(
        _ada_kernel,
        grid=(depth, six_d // bn),
        in_specs=[pl.BlockSpec((SUBLANES, d), lambda l, j: (0, 0)),
                  pl.BlockSpec((1, d, bn), lambda l, j: (l, 0, j)),
                  pl.BlockSpec((1, 1, bn), lambda l, j: (l, 0, j))],
        out_specs=pl.BlockSpec((1, SUBLANES, bn), lambda l, j: (l, 0, j)),
        out_shape=jax.ShapeDtypeStruct((depth, SUBLANES, six_d), F32),
        compiler_params=_cparams(2),
        name="ada",
    )(cond, w_ada, b_ada.reshape(depth, 1, six_d))


def _head_rms(x, ones_bd, g):
    sq = x * x
    hi = sq.astype(BF16)
    lo = (sq - hi.astype(F32)).astype(BF16)
    ssum = _dot(hi, ones_bd) + _dot(lo, ones_bd)
    return x * lax.rsqrt(ssum * (1.0 / HEAD_DIM) + EPS) * g


def _rope(x, cos, sin_signed):
    w = x.shape[-1]
    lane = lax.broadcasted_iota(I32, x.shape, 1)
    first_half = (lane % 32) < 16
    partner = jnp.where(first_half, pltpu.roll(x, w - 16, 1), pltpu.roll(x, 16, 1))
    return x * cos + partner * sin_signed


def _in_proj_kernel(x_ref, mod_ref, g1_ref, w_ref, cos_ref, sin_ref, qg_ref, kg_ref, onesq_ref, onesk_ref,
                    qa_ref, ka_ref, va_ref, qb_ref, kb_ref, vb_ref, u_ref, *, d):
    x = x_ref[...]
    y = x * lax.rsqrt(jnp.mean(x * x, axis=-1, keepdims=True) + EPS) * g1_ref[...]
    shift = mod_ref[:, 0:d]
    scale = mod_ref[:, d:2 * d]
    h = (y * (1.0 + scale) + shift).astype(BF16)
    p = _dot(h, w_ref[...])
    o = 0
    qa = p[:, o:o + A_WIDTH]; o += A_WIDTH
    ka = p[:, o:o + A_KV_WIDTH]; o += A_KV_WIDTH
    va = p[:, o:o + A_KV_WIDTH]; o += A_KV_WIDTH
    qb = p[:, o:o + B_WIDTH]; o += B_WIDTH
    kb = p[:, o:o + B_WIDTH]; o += B_WIDTH
    vb = p[:, o:o + B_WIDTH]; o += B_WIDTH
    cv = p[:, o:o + C_CHANNELS]; o += C_CHANNELS
    cg = p[:, o:o + C_CHANNELS]

    cos = cos_ref[...]
    sin = sin_ref[...]
    sm_scale = HEAD_DIM ** -0.5
    qa = _head_rms(qa, onesq_ref[...], qg_ref[...])
    qa = _rope(qa, jnp.concatenate([cos] * (A_WIDTH // LANES), axis=1),
               jnp.concatenate([sin] * (A_WIDTH // LANES), axis=1)) * (sm_scale * LOG2_E)
    ka = _head_rms(ka, onesk_ref[...], kg_ref[...])
    ka = _rope(ka, cos, sin)
    qa_ref[...] = qa.T.reshape(A_HEADS, HEAD_DIM, -1).astype(BF16)
    va_ref[...] = va.T.reshape(A_KV_HEADS, HEAD_DIM, -1).astype(BF16)
    for hh in range(A_KV_HEADS):
        ka_ref[hh] = ka[:, hh * HEAD_DIM:(hh + 1) * HEAD_DIM].astype(BF16)
    qb_ref[...] = (qb * sm_scale).astype(BF16)
    kb_ref[...] = kb.astype(BF16)
    vb_ref[...] = vb.astype(BF16)
    u_ref[...] = cv * jax.nn.sigmoid(cg)


def _in_proj(x, modsel, g1, w_in, cos, sin, qg, kg, ones_q, ones_k):
    b, s, d = x.shape
    nt = s // TT
    tile = lambda bi, t: (bi, t, 0)
    head_tile = lambda bi, t: (bi, 0, t, 0)
    head_tile_t = lambda bi, t: (bi, 0, 0, t)
    const2 = lambda bi, t: (0, 0)
    return pl.pallas_call(
        functools.partial(_in_proj_kernel, d=d),
        grid=(b, nt),
        in_specs=[pl.BlockSpec((None, TT, d), tile),
                  pl.BlockSpec((None, None, 1, 6 * d), lambda bi, t: (bi, jnp.minimum(t, 1), 0, 0)),
                  pl.BlockSpec((1, d), const2),
                  pl.BlockSpec((d, IN_WIDTH), const2),
                  pl.BlockSpec((TT, LANES), lambda bi, t: (t, 0)),
                  pl.BlockSpec((TT, LANES), lambda bi, t: (t, 0)),
                  pl.BlockSpec((1, A_WIDTH), const2),
                  pl.BlockSpec((1, A_KV_WIDTH), const2),
                  pl.BlockSpec((A_WIDTH, A_WIDTH), const2),
                  pl.BlockSpec((A_KV_WIDTH, A_KV_WIDTH), const2)],
        out_specs=[pl.BlockSpec((None, A_HEADS, HEAD_DIM, TT), head_tile_t),
                   pl.BlockSpec((None, A_KV_HEADS, TT, HEAD_DIM), head_tile),
                   pl.BlockSpec((None, A_KV_HEADS, HEAD_DIM, TT), head_tile_t),
                   pl.BlockSpec((None, TT, B_WIDTH), tile),
                   pl.BlockSpec((None, TT, B_WIDTH), tile),
                   pl.BlockSpec((None, TT, B_WIDTH), tile),
                   pl.BlockSpec((None, TT, C_CHANNELS), tile)],
        out_shape=[jax.ShapeDtypeStruct((b, A_HEADS, HEAD_DIM, s), BF16),
                   jax.ShapeDtypeStruct((b, A_KV_HEADS, s, HEAD_DIM), BF16),
                   jax.ShapeDtypeStruct((b, A_KV_HEADS, HEAD_DIM, s), BF16),
                   jax.ShapeDtypeStruct((b, s, B_WIDTH), BF16),
                   jax.ShapeDtypeStruct((b, s, B_WIDTH), BF16),
                   jax.ShapeDtypeStruct((b, s, B_WIDTH), BF16),
                   jax.ShapeDtypeStruct((b, s, C_CHANNELS), F32)],
        compiler_params=_cparams(2),
        name="in_proj",
    )(x, modsel, g1, w_in, cos, sin, qg, kg, ones_q, ones_k)


def _attn_a_kernel(q_ref, k_ref, v_ref, o_ref, *, ctx, tk, n_chunks):
    t = pl.program_id(2)
    n_heads, dh, tq = q_ref.shape
    hpm = A_HEADS_PER_MATMUL
    g = n_heads // hpm
    qs = [jnp.concatenate([q_ref[hpm * i + h] for h in range(hpm)], axis=1) for i in range(g)]

    def update(state, s, v):
        m, l, acc = state
        m_new = jnp.maximum(m, jnp.max(s, axis=0, keepdims=True))
        alpha = jnp.exp2(m - m_new)
        p = jnp.exp2(s - m_new)
        l = alpha * l + jnp.sum(p, axis=0, keepdims=True)
        return m_new, l, alpha * acc + _dot(v, p.astype(BF16))

    def step(s0, k, v, k_next, states):
        out = []
        s = s0
        for i in range(g):
            s_next = _dot(k, qs[i + 1]) if i + 1 < g else _dot(k_next, qs[0])
            out.append(update(states[i], s, v))
            s = s_next
        return s, tuple(out)

    def chunk(c):
        return pl.ds(pl.multiple_of(ctx + c * tk, int(np.gcd(ctx, tk))), tk)

    init = tuple((jnp.full((1, hpm * tq), NEG, F32), jnp.zeros((1, hpm * tq), F32),
                  jnp.zeros((dh, hpm * tq), F32)) for _ in range(g))
    k_ctx = k_ref[0:ctx, :]
    v_ctx = v_ref[:, 0:ctx]
    s0 = _dot(k_ctx, qs[0])

    def finish(states):
        o_t = jnp.concatenate([(acc / l)[:, h * tq:(h + 1) * tq] for (_, l, acc) in states for h in range(hpm)],
                              axis=0)
        o_ref[...] = o_t.T.astype(o_ref.dtype)

    @pl.when(t == 0)
    def _():
        _, states = step(s0, k_ctx, v_ctx, k_ctx, init)
        finish(states)

    @pl.when(t > 0)
    def _():
        carry = step(s0, k_ctx, v_ctx, k_ref[chunk(0), :], init)

        def body(c, carry):
            s_pair0, states = carry
            nxt = jnp.minimum(c + 1, n_chunks - 1)
            return step(s_pair0, k_ref[chunk(c), :], v_ref[:, chunk(c)], k_ref[chunk(nxt), :], states)

        finish(lax.fori_loop(0, n_chunks, body, carry, unroll=True)[1])


def _attn_a(qa_t, ka, va_t, ctx):
    b, _, dh, s = qa_t.shape
    nt = s // TT
    tk = A_KEY_CHUNK if (s - ctx) % A_KEY_CHUNK == 0 else TT
    return pl.pallas_call(
        functools.partial(_attn_a_kernel, ctx=ctx, tk=tk, n_chunks=(s - ctx) // tk),
        grid=(b, A_KV_HEADS, nt),
        in_specs=[pl.BlockSpec((None, A_GROUP, dh, TT), lambda bi, j, t: (bi, j, 0, t)),
                  pl.BlockSpec((None, None, s, dh), lambda bi, j, t: (bi, j, 0, 0)),
                  pl.BlockSpec((None, None, dh, s), lambda bi, j, t: (bi, j, 0, 0))],
        out_specs=pl.BlockSpec((None, TT, A_GROUP * dh), lambda bi, j, t: (bi, t, j)),
        out_shape=jax.ShapeDtypeStruct((b, s, A_WIDTH), BF16),
        compiler_params=_cparams(3),
        name="attn_a",
    )(qa_t, ka, va_t)


def _attn_b_kernel(q_ref, k_ref, v_ref, bias_ref, o_ref, *, ctx, rows, wr):
    t = pl.program_id(1)
    tt, w = q_ref.shape
    rows_per_tile = tt // GRID_W
    win = wr * GRID_W
    q = q_ref[...]
    kc = k_ref[0:ctx, :]
    vc = v_ref[0:ctx, :]
    lane_head = lax.broadcasted_iota(I32, (tt, w), 1) // HEAD_DIM

    def softmax_parts(parts):
        m = parts[0].max(axis=1, keepdims=True)
        for s in parts[1:]:
            m = jnp.maximum(m, s.max(axis=1, keepdims=True))
        es = [jnp.exp(s - m) for s in parts]
        tot = es[0].sum(axis=1, keepdims=True)
        for e in es[1:]:
            tot = tot + e.sum(axis=1, keepdims=True)
        inv = 1.0 / tot
        return [(e * inv).astype(BF16) for e in es]

    @pl.when(t == 0)
    def _():
        out = jnp.zeros((tt, w), F32)
        for hh in range(B_HEADS):
            qh = jnp.where(lane_head == hh, q, jnp.zeros_like(q))
            (p,) = softmax_parts([_dot_nt(qh, kc)])
            out = out + jnp.where(lane_head == hh, _dot(p, vc), 0.0)
        o_ref[...] = out.astype(o_ref.dtype)

    @pl.when(t > 0)
    def _():
        r0 = (t - 1) * rows_per_tile
        k_rows, v_rows, variant = [], [], []
        for i in range(rows_per_tile):
            r = r0 + i
            rs = jnp.clip(r - wr // 2, 0, rows - wr)
            start = pl.multiple_of(ctx + rs * GRID_W, GRID_W)
            k_rows.append(k_ref[pl.ds(start, win), :])
            v_rows.append(v_ref[pl.ds(start, win), :])
            variant.append(r - rs)
        scores = []
        for hh in range(B_HEADS):
            qh = jnp.where(lane_head == hh, q, jnp.zeros_like(q))
            s_nb = jnp.concatenate([_dot_nt(qh[i * GRID_W:(i + 1) * GRID_W], k_rows[i]) + bias_ref[variant[i], hh]
                                    for i in range(rows_per_tile)], axis=0)
            scores.append([_dot_nt(qh, kc), s_nb])
        probs = [softmax_parts(sc) for sc in scores]
        out = jnp.zeros((tt, w), F32)
        for hh in range(B_HEADS):
            p_cx, p_nb = probs[hh]
            o_nb = jnp.concatenate([_dot(p_nb[i * GRID_W:(i + 1) * GRID_W], v_rows[i])
                                    for i in range(rows_per_tile)], axis=0)
            out = out + jnp.where(lane_head == hh, _dot(p_cx, vc) + o_nb, 0.0)
        o_ref[...] = out.astype(o_ref.dtype)


def _attn_b(qb, kb, vb, bias, ctx):
    b, s, w = qb.shape
    nt = s // TT
    rows = (s - ctx) // GRID_W
    wr = min(NA_WIN_ROWS, rows)
    tile = lambda bi, t: (bi, t, 0)
    whole = lambda bi, t: (bi, 0, 0)
    return pl.pallas_call(
        functools.partial(_attn_b_kernel, ctx=ctx, rows=rows, wr=wr),
        grid=(b, nt),
        in_specs=[pl.BlockSpec((None, TT, w), tile),
                  pl.BlockSpec((None, s, w), whole),
                  pl.BlockSpec((None, s, w), whole),
                  pl.BlockSpec(bias.shape, lambda bi, t: (0, 0, 0, 0))],
        out_specs=pl.BlockSpec((None, TT, w), tile),
        out_shape=jax.ShapeDtypeStruct((b, s, w), BF16),
        compiler_params=_cparams(2),
        name="attn_b",
    )(qb, kb, vb, bias)


def _na_bias_tables(rpb, rows):
    wr = min(NA_WIN_ROWS, rows)
    wc = NA_WIN_COLS
    q = np.arange(GRID_W)
    cs = np.clip(q - wc // 2, 0, GRID_W - wc)
    kc = np.arange(GRID_W)
    valid = (kc[None, :] >= cs[:, None]) & (kc[None, :] < cs[:, None] + wc)
    dc = np.clip(kc[None, :] - q[:, None] + (NA_WIN_COLS - 1), 0, 2 * NA_WIN_COLS - 2)
    v = np.arange(wr)
    wrow = np.arange(wr)
    dr = wrow[None, :] - v[:, None] + (NA_WIN_ROWS - 1)
    pick_r = jnp.asarray(dr[None, :, :] == np.arange(2 * NA_WIN_ROWS - 1)[:, None, None], F32)
    pick_c = jnp.asarray((dc[None, :, :] == np.arange(2 * NA_WIN_COLS - 1)[:, None, None])
                         & valid[None, :, :], F32)
    tab = jnp.einsum("hrc,rvw,cqk->vhqwk", rpb.astype(F32), pick_r, pick_c, precision=lax.Precision.HIGHEST)
    tab = tab + jnp.asarray(np.where(valid, 0.0, NEG), F32)[None, None, :, None, :]
    return tab.reshape(wr, B_HEADS, GRID_W, wr * GRID_W)


def _mix_out_kernel(a_ref, b_ref, up_ref, uc_ref, un_ref, x_ref, mod_ref, cw_ref, cb_ref, lg_ref, lb_ref,
                    wo_ref, g2_ref, wrh_ref, wrl_ref,
                    x1_ref, h2_ref, aff_ref, gate_ref, *, d, nt):
    t = pl.program_id(1)
    tt = uc_ref.shape[0]
    prev_ok = t >= 2
    next_ok = jnp.logical_and(t >= 1, t < nt - 1)
    prev = jnp.where(prev_ok, up_ref[tt - CONV_HALO:tt, :], 0.0)
    nxt = jnp.where(next_ok, un_ref[0:CONV_HALO, :], 0.0)
    win = jnp.concatenate([prev, uc_ref[...], nxt], axis=0)
    off = CONV_HALO - CONV_WIDTH // 2
    acc = jnp.zeros((tt, C_CHANNELS), F32) + cb_ref[...]
    for b in range(SUBLANES):
        part = jnp.zeros((tt + SUBLANES, C_CHANNELS), F32)
        for k in range(CONV_WIDTH):
            if (off + k) % SUBLANES == b:
                a = (off + k) - b
                part = part + win[a:a + tt + SUBLANES, :] * cw_ref[k:k + 1, :]
        acc = acc + part[b:b + tt, :]
    mu = jnp.mean(acc, axis=-1, keepdims=True)
    var = jnp.mean(jnp.square(acc - mu), axis=-1, keepdims=True)
    yc = (acc - mu) * lax.rsqrt(var + EPS) * lg_ref[...] + lb_ref[...]
    c = (yc * jax.nn.sigmoid(yc)).astype(BF16)

    mix = (_dot(a_ref[...], wo_ref[0:A_WIDTH, :])
           + _dot(b_ref[...], wo_ref[A_WIDTH:A_WIDTH + B_WIDTH, :])
           + _dot(c, wo_ref[A_WIDTH + B_WIDTH:MIX_WIDTH, :]))
    g1 = mod_ref[:, 2 * d:3 * d]
    x1 = x_ref[...] + g1 * mix
    x1_ref[...] = x1
    y = x1 * lax.rsqrt(jnp.mean(x1 * x1, axis=-1, keepdims=True) + EPS) * g2_ref[...]
    h2 = y * (1.0 + mod_ref[:, 4 * d:5 * d]) + mod_ref[:, 3 * d:4 * d]
    h2_ref[...] = h2.astype(BF16)
    logits = _dot3(h2, wrh_ref[...], wrl_ref[...])
    lane = lax.broadcasted_iota(I32, logits.shape, 1)
    logits = jnp.where(lane < N_EXPERTS, logits, NEG)
    e = jnp.exp(logits - jnp.max(logits, axis=-1, keepdims=True))
    aff = e / jnp.sum(e, axis=-1, keepdims=True)
    aff_ref[...] = aff.T[0:N_EXPERTS, :]
    hi, mid, lo = _split3(aff)
    gate = jnp.where(lane < N_EXPERTS, hi.astype(F32),
                     jnp.where(lane < 2 * N_EXPERTS, pltpu.roll(mid.astype(F32), N_EXPERTS, 1),
                               jnp.where(lane < 3 * N_EXPERTS, pltpu.roll(lo.astype(F32), 2 * N_EXPERTS, 1), 0.0)))
    gate_ref[...] = gate.astype(BF16)


def _mix_out(a, bmix, u, x, modsel, cw, cb, lg, lb, wo, g2, wr_hi, wr_lo):
    b, s, d = x.shape
    nt = s // TT
    tile = lambda bi, t: (bi, t, 0)
    const2 = lambda bi, t: (0, 0)
    return pl.pallas_call(
        functools.partial(_mix_out_kernel, d=d, nt=nt),
        grid=(b, nt),
        in_specs=[pl.BlockSpec((None, TT, A_WIDTH), tile),
                  pl.BlockSpec((None, TT, B_WIDTH), tile),
                  pl.BlockSpec((None, TT, C_CHANNELS), lambda bi, t: (bi, jnp.maximum(t - 1, 0), 0)),
                  pl.BlockSpec((None, TT, C_CHANNELS), tile),
                  pl.BlockSpec((None, TT, C_CHANNELS), lambda bi, t: (bi, jnp.minimum(t + 1, nt - 1), 0)),
                  pl.BlockSpec((None, TT, d), tile),
                  pl.BlockSpec((None, None, 1, 6 * d), lambda bi, t: (bi, jnp.minimum(t, 1), 0, 0)),
                  pl.BlockSpec((CONV_WIDTH, C_CHANNELS), const2),
                  pl.BlockSpec((1, C_CHANNELS), const2),
                  pl.BlockSpec((1, C_CHANNELS), const2),
                  pl.BlockSpec((1, C_CHANNELS), const2),
                  pl.BlockSpec((MIX_WIDTH, d), const2),
                  pl.BlockSpec((1, d), const2),
                  pl.BlockSpec((d, LANES), const2),
                  pl.BlockSpec((d, LANES), const2)],
        out_specs=[pl.BlockSpec((None, TT, d), tile),
                   pl.BlockSpec((None, TT, d), tile),
                   pl.BlockSpec((None, N_EXPERTS, TT), lambda bi, t: (bi, 0, t)),
                   pl.BlockSpec((None, TT, LANES), tile)],
        out_shape=[jax.ShapeDtypeStruct((b, s, d), F32),
                   jax.ShapeDtypeStruct((b, s, d), BF16),
                   jax.ShapeDtypeStruct((b, N_EXPERTS, s), F32),
                   jax.ShapeDtypeStruct((b, s, LANES), BF16)],
        compiler_params=_cparams(2),
        name="mix_out",
    )(a, bmix, u, u, u, x, modsel, cw, cb, lg, lb, wo, g2, wr_hi, wr_lo)


def _route_kernel(aff_ref, tri_ref, pos_ref, post_ref, start_ref, cnt_ref, *, segments):
    bi = pl.program_id(0)
    tri = tri_ref[...]
    ne = N_EXPERTS
    lane_t = lax.broadcasted_iota(I32, (ne, LANES), 1)
    starts = jnp.zeros((ne, LANES), F32)
    cnts = jnp.zeros((ne, LANES), F32)

    def block_cumsum(mask_f32, carry):
        inc = _dot(mask_f32.astype(BF16), tri) + carry
        return inc - mask_f32, inc[:, TT - 1:TT]

    for (tile0, ntiles, cap, row0, rows_per_sample) in segments:
        lo = tile0 * TT
        n = ntiles * TT
        bits = pltpu.bitcast(aff_ref[:, lo:lo + n], I32)

        def search(i, cur):
            cand = cur | lax.shift_left(jnp.int32(1), 30 - i)
            c = jnp.sum(jnp.where(bits >= cand, 1.0, 0.0), axis=1, keepdims=True)
            return jnp.where(c >= cap, cand, cur)

        thr = lax.fori_loop(0, 31, search, jnp.zeros((ne, 1), I32))
        n_gt = jnp.sum(jnp.where(bits > thr, 1.0, 0.0), axis=1, keepdims=True)
        need = cap - n_gt
        base = (row0 + bi * rows_per_sample).astype(F32)
        tie_carry = jnp.zeros((ne, 1), F32)
        pos_carry = jnp.zeros((ne, 1), F32)
        for k in range(ntiles):
            bk = bits[:, k * TT:(k + 1) * TT]
            eq = jnp.where(bk == thr, 1.0, 0.0)
            tie_rank, tie_carry = block_cumsum(eq, tie_carry)
            sel = jnp.where(bk > thr, 1.0, jnp.where(tie_rank < need, eq, 0.0))
            rank, new_carry = block_cumsum(sel, pos_carry)
            pos = jnp.where(sel > 0.0, rank + base, -1.0)
            pos_ref[:, lo + k * TT:lo + (k + 1) * TT] = pos.astype(I32)
            padded = jnp.concatenate([pos, jnp.full((LANES - ne, TT), -1.0, F32)], axis=0)
            post_ref[lo + k * TT:lo + (k + 1) * TT, :] = padded.T.astype(I32)
            starts = jnp.where(lane_t == tile0 + k, pos_carry + base, starts)
            cnts = jnp.where(lane_t == tile0 + k, new_carry - pos_carry, cnts)
            pos_carry = new_carry
    start_ref[...] = starts.astype(I32)
    cnt_ref[...] = cnts.astype(I32)


def _route(aff_t, tri, segments):
    b, ne, s = aff_t.shape
    return pl.pallas_call(
        functools.partial(_route_kernel, segments=segments),
        grid=(b,),
        in_specs=[pl.BlockSpec((None, ne, s), lambda bi: (bi, 0, 0)),
                  pl.BlockSpec((TT, TT), lambda bi: (0, 0))],
        out_specs=[pl.BlockSpec((None, ne, s), lambda bi: (bi, 0, 0)),
                   pl.BlockSpec((None, s, LANES), lambda bi: (bi, 0, 0)),
                   pl.BlockSpec((None, ne, LANES), lambda bi: (bi, 0, 0)),
                   pl.BlockSpec((None, ne, LANES), lambda bi: (bi, 0, 0))],
        out_shape=[jax.ShapeDtypeStruct((b, ne, s), I32),
                   jax.ShapeDtypeStruct((b, s, LANES), I32),
                   jax.ShapeDtypeStruct((b, ne, LANES), I32),
                   jax.ShapeDtypeStruct((b, ne, LANES), I32)],
        compiler_params=_cparams(1),
        name="route",
    )(aff_t, tri)


def _step_to_tile(step, n_lat_steps, lat_tiles):
    is_lat = step < n_lat_steps
    bi = jnp.where(is_lat, step // lat_tiles, step - n_lat_steps)
    t = jnp.where(is_lat, 1 + step % lat_tiles, 0)
    return bi, t


def _dispatch_kernel(start_sm, cnt_sm, h_ref, pos_ref, gate_ref, xe_ref, xbuf, xextra, carry, sems,
                     *, n_lat_steps, lat_tiles, nt, d, slot_rows):
    step = pl.program_id(0)
    last_step = pl.num_programs(0) - 1
    slot = step % 2
    bi, t = _step_to_tile(step, n_lat_steps, lat_tiles)
    ne = N_EXPERTS
    rnd = DISPATCH_ROUND
    sub = SUBLANES
    width = xextra.shape[1]

    @pl.when(step == 0)
    def _():
        carry[...] = jnp.zeros_like(carry)
        xextra[0:rnd, :] = jnp.zeros((rnd, width), F32)
        pads = [pltpu.make_async_copy(xextra.at[pl.ds(0, rnd)], xe_ref.at[e, pl.ds(slot_rows, rnd)],
                                      sems.at[2, e]) for e in range(ne)]
        for cp in pads:
            cp.start()
        for cp in pads:
            cp.wait()

    sbase = (bi * nt + t) * ne
    starts = [start_sm[sbase + e] for e in range(ne)]
    cnts = [cnt_sm[sbase + e] for e in range(ne)]
    ends = [starts[e] + cnts[e] for e in range(ne)]
    a0 = [(starts[e] // sub) * sub for e in range(ne)]
    last_grp = [(ends[e] // sub) * sub for e in range(ne)]
    n_rounds = jnp.int32(0)
    for e in range(ne):
        n_rounds = jnp.maximum(n_rounds, jnp.where(cnts[e] > 0, (ends[e] - a0[e] + rnd - 1) // rnd, 0))
    h = h_ref[...]
    gates = gate_ref[...]
    pos = pos_ref[...]
    row = lax.broadcasted_iota(I32, (rnd, TT), 0)
    grp_row = lax.broadcasted_iota(I32, (sub, width), 0)

    def copy(e, r, buf, sem_row):
        dst = pl.ds(pl.multiple_of(a0[e] + r * rnd, sub), rnd)
        return pltpu.make_async_copy(buf.at[pl.ds(e * rnd, rnd)], xe_ref.at[e, dst], sems.at[sem_row, e])

    def active(e, r):
        return jnp.logical_and(cnts[e] > 0, a0[e] + r * rnd < ends[e])

    def fill(r, buf):
        sel = jnp.concatenate(
            [jnp.where(pos[e:e + 1, :] - (a0[e] + r * rnd) == row, 1.0, 0.0).astype(BF16)
             for e in range(ne)], axis=0)
        buf[:, 0:d] = _dot(sel, h)
        buf[:, d:] = _dot(sel, gates)
        if r == 0:
            for e in range(ne):
                buf[e * rnd:e * rnd + sub, :] += carry[e]
        for e in range(ne):
            partial = jnp.logical_and(ends[e] > last_grp[e], (last_grp[e] - a0[e]) // rnd == r)

            @pl.when(jnp.logical_and(cnts[e] > 0, partial))
            def _():
                off = pl.multiple_of(e * rnd + (last_grp[e] - a0[e]) - r * rnd, sub)
                carry[e] = jnp.where(grp_row < ends[e] - last_grp[e], buf[pl.ds(off, sub), :], 0.0)

    def later_round(r):
        fill(r, xextra)
        for e in range(ne):
            pl.when(active(e, r))(lambda e=e: copy(e, r, xextra, 2).start())
        for e in range(ne):
            pl.when(active(e, r))(lambda e=e: copy(e, r, xextra, 2).wait())

    pl.when(n_rounds > 0)(lambda: fill(0, xbuf.at[slot]))

    @pl.when(step > 0)
    def _():
        pbi, pt = _step_to_tile(jnp.maximum(step - 1, 0), n_lat_steps, lat_tiles)
        pbase = (pbi * nt + pt) * ne
        for e in range(ne):
            @pl.when(cnt_sm[pbase + e] > 0)
            def _():
                pltpu.make_async_copy(xbuf.at[1 - slot, pl.ds(e * rnd, rnd)], xe_ref.at[e, pl.ds(0, rnd)],
                                      sems.at[1 - slot, e]).wait()

    for e in range(ne):
        pl.when(cnts[e] > 0)(lambda e=e: copy(e, 0, xbuf.at[slot], slot).start())
    for r in range(1, (TT + sub - 1 + rnd - 1) // rnd):
        pl.when(r < n_rounds)(functools.partial(later_round, r))
    for e in range(ne):
        @pl.when(jnp.logical_and(cnts[e] > 0, ends[e] == last_grp[e]))
        def _():
            carry[e] = jnp.zeros((sub, width), F32)

    @pl.when(step == last_step)
    def _():
        for e in range(ne):
            pl.when(cnts[e] > 0)(lambda e=e: copy(e, 0, xbuf.at[slot], slot).wait())


def _dispatch(starts, cnts, h2, pos, gate3, slot_rows):
    b, s, d = h2.shape
    nt = s // TT
    lat_tiles = nt - 1
    n_lat_steps = b * lat_tiles
    steps = n_lat_steps + b
    decode = functools.partial(_step_to_tile, n_lat_steps=n_lat_steps, lat_tiles=lat_tiles)

    def tile(step, *_):
        bi, t = decode(step)
        return bi, t, 0

    def pos_tile(step, *_):
        bi, t = decode(step)
        return bi, 0, t

    rows = N_EXPERTS * DISPATCH_ROUND
    width = d + LANES
    return pl.pallas_call(
        functools.partial(_dispatch_kernel, n_lat_steps=n_lat_steps, lat_tiles=lat_tiles, nt=nt, d=d,
                          slot_rows=slot_rows),
        grid_spec=pltpu.PrefetchScalarGridSpec(
            num_scalar_prefetch=2,
            grid=(steps,),
            in_specs=[pl.BlockSpec((None, TT, d), tile),
                      pl.BlockSpec((None, N_EXPERTS, TT), pos_tile),
                      pl.BlockSpec((None, TT, LANES), tile)],
            out_specs=pl.BlockSpec(memory_space=pl.ANY),
            scratch_shapes=[pltpu.VMEM((2, rows, width), F32),
                            pltpu.VMEM((rows, width), F32),
                            pltpu.VMEM((N_EXPERTS, SUBLANES, width), F32),
                            pltpu.SemaphoreType.DMA((3, N_EXPERTS))]),
        out_shape=jax.ShapeDtypeStruct((N_EXPERTS, slot_rows + DISPATCH_ROUND, width), F32),
        compiler_params=_cparams(1),
        name="dispatch",
    )(starts, cnts, h2, pos, gate3)


def _ffn_kernel(x_ref, wg_hbm, wu_hbm, wd_hbm, y_ref, wg_bf, wu_bf, wd_bf, sg, su, sd, sems, *, d, layer, n_tiles):
    e = pl.program_id(0)
    j = pl.program_id(1)
    n_exp = pl.num_programs(0)
    cur = e % 2
    par = (e * n_tiles + j) % 2
    rows_g = sg.shape[1]
    rows_d = sd.shape[1]

    def slab_copies(expert, slab, buf):
        return (pltpu.make_async_copy(wg_hbm.at[layer, expert, pl.ds(slab * rows_g, rows_g)], sg.at[buf],
                                      sems.at[0, buf]),
                pltpu.make_async_copy(wu_hbm.at[layer, expert, pl.ds(slab * rows_g, rows_g)], su.at[buf],
                                      sems.at[1, buf]),
                pltpu.make_async_copy(wd_hbm.at[layer, expert, pl.ds(slab * rows_d, rows_d)], sd.at[buf],
                                      sems.at[2, buf]))

    def cast_slab(half, slab, buf):
        og = pl.multiple_of(slab * rows_g, BF16_ROWS)
        od = pl.multiple_of(slab * rows_d, BF16_ROWS)
        wg_bf[half, pl.ds(og, rows_g), :] = sg[buf].astype(BF16)
        wu_bf[half, pl.ds(og, rows_g), :] = su[buf].astype(BF16)
        wd_bf[half, pl.ds(od, rows_d), :] = sd[buf].astype(BF16)

    @pl.when(jnp.logical_and(e == 0, j == 0))
    def _():
        for slab in range(n_tiles):
            cps = slab_copies(0, slab, 0)
            for cp in cps:
                cp.start()
            for cp in cps:
                cp.wait()
            cast_slab(0, slab, 0)
        for cp in slab_copies(0, n_tiles - 1, 1):
            cp.start()

    e_prev = jnp.where(j > 0, e, e - 1)
    j_prev = jnp.where(j > 0, j - 1, n_tiles - 1)
    src_prev = jnp.minimum(e_prev + 1, n_exp - 1)
    for cp in slab_copies(src_prev, j_prev, 1 - par):
        cp.wait()
    for cp in slab_copies(jnp.minimum(e + 1, n_exp - 1), j, par):
        cp.start()
    cast_slab((e_prev + 1) % 2, j_prev, 1 - par)

    x = x_ref[:, 0:d].astype(BF16)
    gt = _dot(x, wg_bf[cur])
    up = _dot(x, wu_bf[cur])
    hid = (gt * jax.nn.sigmoid(gt) * up).astype(BF16)
    y = _dot(hid, wd_bf[cur])
    g3 = x_ref[:, d:]
    lane = lax.broadcasted_iota(I32, g3.shape, 1)
    mine = jnp.logical_and(lane % N_EXPERTS == e, lane < 3 * N_EXPERTS)
    gate = jnp.sum(jnp.where(mine, g3, 0.0), axis=1, keepdims=True)
    y_ref[...] = (y * gate).astype(y_ref.dtype)

    @pl.when(jnp.logical_and(e == n_exp - 1, j == n_tiles - 1))
    def _():
        for cp in slab_copies(n_exp - 1, j, par):
            cp.wait()


def _ffn_tile(rows):
    best = BF16_ROWS
    for m in range(BF16_ROWS, 641, BF16_ROWS):
        if rows % m == 0:
            best = m
    return best


def _ffn(xe, w_gate, w_up, w_down, layer, slot_rows):
    ne, _, width = xe.shape
    d, hdim = w_gate.shape[-2:]
    mt = _ffn_tile(slot_rows)
    n_tiles = slot_rows // mt
    assert d % (n_tiles * BF16_ROWS) == 0 and hdim % (n_tiles * BF16_ROWS) == 0
    return pl.pallas_call(
        functools.partial(_ffn_kernel, d=d, layer=layer, n_tiles=n_tiles),
        grid=(ne, n_tiles),
        in_specs=[pl.BlockSpec((None, mt, width), lambda e, j: (e, j, 0)),
                  pl.BlockSpec(memory_space=pl.ANY),
                  pl.BlockSpec(memory_space=pl.ANY),
                  pl.BlockSpec(memory_space=pl.ANY)],
        out_specs=pl.BlockSpec((None, mt, d), lambda e, j: (e, j, 0)),
        out_shape=jax.ShapeDtypeStruct((ne, slot_rows, d), BF16),
        scratch_shapes=[pltpu.VMEM((2, d, hdim), BF16),
                        pltpu.VMEM((2, d, hdim), BF16),
                        pltpu.VMEM((2, hdim, d), BF16),
                        pltpu.VMEM((2, d // n_tiles, hdim), F32),
                        pltpu.VMEM((2, d // n_tiles, hdim), F32),
                        pltpu.VMEM((2, hdim // n_tiles, d), F32),
                        pltpu.SemaphoreType.DMA((3, 2))],
        compiler_params=_cparams(2),
        name="ffn",
    )(xe, w_gate, w_up, w_down)


def _combine_kernel(start_sm, cnt_sm, x_ref, post_ref, mod_ref, fg_ref, y_ref, o_ref, ybuf, yextra, acc_ref, sems,
                    *, d, nt, slot_rows, final):
    t0 = 1 if final else 0
    bi = pl.program_id(0)
    tt = pl.program_id(1)
    tiles = pl.num_programs(1)
    lin = bi * tiles + tt
    slot = lin % 2
    ne = N_EXPERTS
    win = COMBINE_WINDOW
    sbase = (bi * nt + tt + t0) * ne
    starts = [start_sm[sbase + e] for e in range(ne)]
    cmax = cnt_sm[sbase]
    for e in range(1, ne):
        cmax = jnp.maximum(cmax, cnt_sm[sbase + e])
    post = post_ref[...]
    lane = lax.broadcasted_iota(I32, (TT, win), 1)

    def window_rows(first_slots, r):
        return [jnp.minimum((first_slots[e] // BF16_ROWS) * BF16_ROWS + r * COMBINE_ROUND, slot_rows - win)
                for e in range(ne)]

    def fetch(rows, dst, sem_row):
        return [pltpu.make_async_copy(y_ref.at[e, pl.ds(pl.multiple_of(rows[e], BF16_ROWS), win)],
                                      dst.at[pl.ds(e * win, win)], sems.at[sem_row, e]) for e in range(ne)]

    def add_round(r, rows, src):
        sel = []
        for e in range(ne):
            rel = post[:, e:e + 1] - starts[e]
            in_round = jnp.logical_and(rel >= r * COMBINE_ROUND, rel < (r + 1) * COMBINE_ROUND)
            hit = jnp.logical_and(in_round, post[:, e:e + 1] - rows[e] == lane)
            sel.append(jnp.where(hit, 1.0, 0.0).astype(BF16))
        acc_ref[...] += _dot(jnp.concatenate(sel, axis=1), src[...])

    rows0 = window_rows(starts, 0)

    @pl.when(lin == 0)
    def _():
        for cp in fetch(rows0, ybuf.at[slot], slot):
            cp.start()

    @pl.when(lin + 1 < pl.num_programs(0) * tiles)
    def _():
        nxt = lin + 1
        nbase = ((nxt // tiles) * nt + nxt % tiles + t0) * ne
        for cp in fetch(window_rows([start_sm[nbase + e] for e in range(ne)], 0), ybuf.at[1 - slot], 1 - slot):
            cp.start()

    acc_ref[...] = jnp.zeros_like(acc_ref)
    for cp in fetch(rows0, ybuf.at[slot], slot):
        cp.wait()
    add_round(0, rows0, ybuf.at[slot])

    def later_round(r):
        rows = window_rows(starts, r)
        cps = fetch(rows, yextra, 2)
        for cp in cps:
            cp.start()
        for cp in cps:
            cp.wait()
        add_round(r, rows, yextra)

    n_rounds = (cmax + COMBINE_ROUND - 1) // COMBINE_ROUND
    for r in range(1, (TT + COMBINE_ROUND - 1) // COMBINE_ROUND):
        pl.when(r < n_rounds)(functools.partial(later_round, r))
    x2 = x_ref[...] + mod_ref[:, 5 * d:6 * d] * acc_ref[...]
    if final:
        x2 = x2 * lax.rsqrt(jnp.mean(x2 * x2, axis=-1, keepdims=True) + EPS) * fg_ref[...]
    o_ref[...] = x2


def _combine(starts, cnts, x1, pos_t, modsel, final_g, y, slot_rows, final):
    b, s, d = x1.shape
    nt = s // TT
    t0 = 1 if final else 0
    tile = lambda bi, t, *_: (bi, t + t0, 0)
    out_rows = s - t0 * TT if final else s
    out_tile = (lambda bi, t, *_: (bi, t, 0)) if final else tile
    return pl.pallas_call(
        functools.partial(_combine_kernel, d=d, nt=nt, slot_rows=slot_rows, final=final),
        grid_spec=pltpu.PrefetchScalarGridSpec(
            num_scalar_prefetch=2,
            grid=(b, nt - t0),
            in_specs=[pl.BlockSpec((None, TT, d), tile),
                      pl.BlockSpec((None, TT, LANES), tile),
                      pl.BlockSpec((None, None, 1, 6 * d), lambda bi, t, *_: (bi, jnp.minimum(t + t0, 1), 0, 0)),
                      pl.BlockSpec((1, d), lambda bi, t, *_: (0, 0)),
                      pl.BlockSpec(memory_space=pl.ANY)],
            out_specs=pl.BlockSpec((None, TT, d), out_tile),
            scratch_shapes=[pltpu.VMEM((2, N_EXPERTS * COMBINE_WINDOW, d), BF16),
                            pltpu.VMEM((N_EXPERTS * COMBINE_WINDOW, d), BF16),
                            pltpu.VMEM((TT, d), F32),
                            pltpu.SemaphoreType.DMA((3, N_EXPERTS))]),
        out_shape=jax.ShapeDtypeStruct((b, out_rows, d), F32),
        compiler_params=_cparams(2),
        name="combine",
    )(starts, cnts, x1, pos_t, modsel, final_g, y)


def _rope_tables(n, ctx):
    tkn = np.arange(n)
    half = HEAD_DIM // 4
    inv = ROPE_THETA ** (-np.arange(half, dtype=np.float64) / half)
    ang_row = (tkn // GRID_W)[:, None] * inv[None, :]
    ang_col = (tkn % GRID_W)[:, None] * inv[None, :]
    ang = np.concatenate([ang_row, ang_row, ang_col, ang_col], axis=1)
    sign = np.tile(np.concatenate([-np.ones(half), np.ones(half)]), 2)
    cos = np.concatenate([np.ones((ctx, HEAD_DIM)), np.cos(ang)], axis=0)
    sin = np.concatenate([np.zeros((ctx, HEAD_DIM)), np.sin(ang) * sign[None, :]], axis=0)
    reps = LANES // HEAD_DIM
    return (jnp.asarray(np.tile(cos, (1, reps)), F32), jnp.asarray(np.tile(sin, (1, reps)), F32))


def _block_diag_ones(width):
    idx = np.arange(width) // HEAD_DIM
    return jnp.asarray(idx[:, None] == idx[None, :], BF16)


def kernel(x, c, ctx, c_ctx, w_ada, b_ada, norm1_g, norm2_g, w_in, q_norm_g, k_norm_g, na_rpb, conv_w, conv_b,
           conv_ln_g, conv_ln_b, w_out, w_router, w_gate, w_up, w_down, final_norm_g):
    b, n, d = x.shape
    nctx = ctx.shape[1]
    depth = w_ada.shape[0]
    assert nctx == TT and n % TT == 0 and n % GRID_W == 0 and b <= SUBLANES - 1
    assert w_router.shape[-1] == N_EXPERTS and w_in.shape[-1] == IN_WIDTH
    s = nctx + n
    nt = s // TT
    cap_lat = CAPACITY_FACTOR * n // N_EXPERTS
    cap_ctx = CAPACITY_FACTOR * nctx // N_EXPERTS
    assert cap_ctx <= DISPATCH_ROUND and cap_lat % BF16_ROWS == 0
    lat_rows = b * cap_lat
    ctx_rows = b * DISPATCH_ROUND

    cond = jnp.zeros((SUBLANES, d), F32).at[:b].set(c).at[b].set(c_ctx)
    mods = _ada(cond, w_ada, b_ada)
    cos, sin = _rope_tables(n, nctx)
    ones_q = _block_diag_ones(A_WIDTH)
    ones_k = _block_diag_ones(A_KV_WIDTH)
    tri = jnp.asarray(np.triu(np.ones((TT, TT))), BF16)

    stream = jnp.concatenate([ctx, x], axis=1)
    slot_rows = lat_rows + ctx_rows
    segments = ((0, 1, cap_ctx, lat_rows, DISPATCH_ROUND), (1, nt - 1, cap_lat, 0, cap_lat))
    bias_rows = n // GRID_W
    for i in range(depth):
        m = mods[i]
        modsel = jnp.stack([jnp.broadcast_to(m[b], (b, 6 * d)), m[:b]], axis=1)[:, :, None, :]
        qg = jnp.tile(q_norm_g[i], A_HEADS)[None, :]
        kg = jnp.tile(k_norm_g[i], A_KV_HEADS)[None, :]
        qa, ka, va, qb, kb, vb, u = _in_proj(stream, modsel, norm1_g[i][None, :], w_in[i].astype(BF16),
                                             cos, sin, qg, kg, ones_q, ones_k)
        a_mix = _attn_a(qa, ka, va, nctx)
        b_mix = _attn_b(qb, kb, vb, _na_bias_tables(na_rpb[i], bias_rows), nctx)
        wr = jnp.zeros((d, LANES), F32).at[:, :N_EXPERTS].set(w_router[i])
        wr_hi = wr.astype(BF16)
        wr_lo = (wr - wr_hi.astype(F32)).astype(BF16)
        x1, h2, aff_t, gate3 = _mix_out(a_mix, b_mix, u, stream, modsel, conv_w[i], conv_b[i][None, :],
                                        conv_ln_g[i][None, :], conv_ln_b[i][None, :], w_out[i].astype(BF16),
                                        norm2_g[i][None, :], wr_hi, wr_lo)
        pos, pos_t, start_t, cnt_t = _route(aff_t, tri, segments)
        starts = jnp.transpose(start_t, (0, 2, 1))[:, :nt, :].reshape(-1)
        cnts = jnp.transpose(cnt_t, (0, 2, 1))[:, :nt, :].reshape(-1)
        xe = _dispatch(starts, cnts, h2, pos, gate3, slot_rows)
        y = _ffn(xe, w_gate, w_up, w_down, i, slot_rows)
        stream = _combine(starts, cnts, x1, pos_t, modsel, final_norm_g[None, :], y, slot_rows, i == depth - 1)
    return stream
```
